```python
import math
import jax, jax.numpy as jnp
from jax import lax
import numpy as np

D_MODEL = 1024
BATCH = 4
SEQ = 8192
DEPTH = 4

N_MIXERS = 2
N_ATTN_LAYERS = (DEPTH + 1) // 2
N_MLSTM_LAYERS = DEPTH // 2

ATTN_HEADS = 16
ATTN_HEAD_DIM = D_MODEL // ATTN_HEADS
DILATION_PATTERNS = ((128, 1), (512, 4), (2048, 16))
N_GROUPS = len(DILATION_PATTERNS)
ATTN_GROUP_WIDTH = ATTN_HEADS * ATTN_HEAD_DIM
ATTN_IN_WIDTH = 3 * N_GROUPS * ATTN_GROUP_WIDTH
REL_BUCKETS = 32
REL_MAX_DISTANCE = 2048

MLSTM_HEADS = 4
MLSTM_V_DIM = D_MODEL // MLSTM_HEADS
MLSTM_QK_DIM = MLSTM_V_DIM // 2
MLSTM_CHUNK = 64
CONV_WIDTH = 4
MLSTM_QK_WIDTH = 2 * MLSTM_HEADS * MLSTM_QK_DIM
MLSTM_VW = MLSTM_HEADS * MLSTM_V_DIM
MLSTM_IN_WIDTH = MLSTM_QK_WIDTH + MLSTM_VW + D_MODEL + 2 * MLSTM_HEADS

D_FF = 4 * D_MODEL
EPS = 1e-6

kernel_name = "hybrid_dilated_attn_mlstm_trunk"


def rms_norm(x, g):
    xf = x.astype(jnp.float32)
    y = xf * lax.rsqrt(jnp.mean(xf * xf, axis=-1, keepdims=True) + EPS)
    return (y * g.astype(jnp.float32)).astype(x.dtype)


def t5_bucket(dist):
    max_exact = REL_BUCKETS // 2
    d = jnp.maximum(dist.astype(jnp.float32), 1.0)
    large = max_exact + (jnp.log(d / max_exact) / math.log(REL_MAX_DISTANCE / max_exact)
                         * (REL_BUCKETS - max_exact)).astype(jnp.int32)
    large = jnp.minimum(large, REL_BUCKETS - 1)
    return jnp.where(dist < max_exact, dist, large)


def dilated_window_attention(q, k, v, rel_bias, window, dilation):
    b, s, h, e = q.shape
    band = window // dilation
    span = band * dilation
    s_pad = -(-s // span) * span
    n_blk = s_pad // span

    def to_blocks(t):
        t = jnp.pad(t, ((0, 0), (0, s_pad - s), (0, 0), (0, 0)))
        return t.reshape(b, n_blk, band, dilation, h, e)

    def with_prev(t):
        prev = jnp.pad(t[:, :-1], ((0, 0), (1, 0), (0, 0), (0, 0), (0, 0), (0, 0)))
        return jnp.concatenate([prev, t], axis=2)

    qb = to_blocks(q)
    kk = with_prev(to_blocks(k))
    vv = with_prev(to_blocks(v))

    scores = jnp.einsum('bnqrhe,bnkrhe->bnrhqk', qb, kk)
    qi = jnp.arange(band)[:, None]
    ki = jnp.arange(2 * band)[None, :]
    steps = qi - ki + band
    valid = (steps >= 0) & (steps <= band)
    blk_valid = valid[None] & ((jnp.arange(n_blk)[:, None, None] > 0) | (ki >= band)[None])
    bias = jnp.transpose(rel_bias[t5_bucket(jnp.clip(steps, 0, band) * dilation)], (2, 0, 1))
    scores = scores + bias.astype(jnp.float32)[None, None, None]
    scores = jnp.where(blk_valid[None, :, None, None], scores, -jnp.inf)

    m = jnp.max(scores, axis=-1, keepdims=True)
    ex = jnp.exp(scores - m)
    den = jnp.sum(ex, axis=-1, keepdims=True)
    out = jnp.einsum('bnrhqk,bnkrhe->bnqrhe', ex / den, vv)
    lse = (m + jnp.log(den))[..., 0]
    out = out.reshape(b, s_pad, h, e)[:, :s]
    lse = jnp.transpose(lse, (0, 1, 4, 2, 3)).reshape(b, s_pad, h)[:, :s]
    return out, lse


def attention_mixer(x, w_in, q_gain, k_gain, w_out, rel_bias):
    b, s, _ = x.shape
    qkv = (x @ w_in).astype(jnp.float32).reshape(b, s, 3, N_GROUPS, ATTN_HEADS, ATTN_HEAD_DIM)
    outs, lses = [], []
    for g, (window, dilation) in enumerate(DILATION_PATTERNS):
        q = rms_norm(qkv[:, :, 0, g], q_gain[g]) * (ATTN_HEAD_DIM ** -0.5)
        k = rms_norm(qkv[:, :, 1, g], k_gain[g])
        o, l = dilated_window_attention(q, k, qkv[:, :, 2, g], rel_bias[:, g], window, dilation)
        outs.append(o)
        lses.append(l)
    wts = jax.nn.softmax(jnp.stack(lses), axis=0)
    y = jnp.einsum('gbsh,gbshe->bshe', wts, jnp.stack(outs))
    return y.reshape(b, s, ATTN_GROUP_WIDTH).astype(x.dtype) @ w_out


def causal_depthwise_conv(x, w, bias):
    y = lax.conv_general_dilated(x, w[:, None, :].astype(x.dtype), window_strides=(1,),
                                 padding=[(CONV_WIDTH - 1, 0)],
                                 dimension_numbers=('NWC', 'WIO', 'NWC'),
                                 feature_group_count=x.shape[-1])
    return y + bias.astype(x.dtype)


def mlstm_chunkwise(q, k, v, log_i, log_f):
    b, h, s, dk = q.shape
    dv = v.shape[-1]
    L = MLSTM_CHUNK
    nc = s // L

    def chunks(t):
        return jnp.moveaxis(t.reshape(b, h, nc, L, *t.shape[3:]), 2, 0)

    causal = jnp.tril(jnp.ones((L, L), dtype=bool))

    def step(carry, xs):
        c_prev, n_prev, m_prev = carry
        qc, kc, vc, ic, fc = xs
        bcum = jnp.cumsum(fc, axis=-1)
        d = jnp.where(causal, bcum[..., :, None] - bcum[..., None, :] + ic[..., None, :], -jnp.inf)
        m_t = jnp.maximum(bcum + m_prev[..., None], jnp.max(d, axis=-1))
        inter = jnp.exp(bcum + m_prev[..., None] - m_t)
        wts = jnp.einsum('bhtd,bhsd->bhts', qc, kc) * jnp.exp(d - m_t[..., None])
        num = inter[..., None] * jnp.einsum('bhvd,bhtd->bhtv', c_prev, qc) + jnp.einsum('bhts,bhsv->bhtv', wts, vc)
        den = inter * jnp.einsum('bhd,bhtd->bht', n_prev, qc) + jnp.sum(wts, axis=-1)
        hc = num / jnp.maximum(jnp.abs(den), jnp.exp(-m_t))[..., None]
        b_last = bcum[..., -1]
        a = b_last[..., None] - bcum + ic
        m_new = jnp.maximum(b_last + m_prev, jnp.max(a, axis=-1))
        decay = jnp.exp(b_last + m_prev - m_new)
        wk = jnp.exp(a - m_new[..., None])
        c_new = decay[..., None, None] * c_prev + jnp.einsum('bhs,bhsv,bhsd->bhvd', wk, vc, kc)
        n_new = decay[..., None] * n_prev + jnp.einsum('bhs,bhsd->bhd', wk, kc)
        return (c_new, n_new, m_new), hc

    init = (jnp.zeros((b, h, dv, dk), jnp.float32), jnp.zeros((b, h, dk), jnp.float32),
            jnp.zeros((b, h), jnp.float32))
    _, hs = lax.scan(step, init, (chunks(q), chunks(k), chunks(v), chunks(log_i), chunks(log_f)))
    return jnp.moveaxis(hs, 0, 2).reshape(b, h, s, dv)


def mlstm_mixer(x, w_in, gate_bias, conv_w, conv_b, out_gain, w_out):
    b, s, _ = x.shape
    hh = MLSTM_HEADS
    proj = x @ w_in
    qk, v, o_pre, gates = jnp.split(proj, [MLSTM_QK_WIDTH, MLSTM_QK_WIDTH + MLSTM_VW,
                                           MLSTM_QK_WIDTH + MLSTM_VW + D_MODEL], axis=-1)
    qk = jax.nn.silu(causal_depthwise_conv(qk, conv_w, conv_b))
    q, k = jnp.split(qk, 2, axis=-1)
    gates = gates.astype(jnp.float32) + gate_bias.astype(jnp.float32)
    log_i = jnp.transpose(gates[..., :hh], (0, 2, 1))
    log_f = jnp.transpose(jax.nn.log_sigmoid(gates[..., hh:]), (0, 2, 1))

    def heads(t, dh):
        return jnp.transpose(t.astype(jnp.float32).reshape(b, s, hh, dh), (0, 2, 1, 3))

    hcell = mlstm_chunkwise(heads(q, MLSTM_QK_DIM) * (MLSTM_QK_DIM ** -0.5), heads(k, MLSTM_QK_DIM),
                            heads(v, MLSTM_V_DIM), log_i, log_f)
    hcell = rms_norm(jnp.transpose(hcell, (0, 2, 1, 3)), out_gain)
    y = hcell.reshape(b, s, MLSTM_VW) * jax.nn.sigmoid(o_pre.astype(jnp.float32))
    return y.astype(x.dtype) @ w_out


def sq_relu_mlp(x, w1, w2):
    hid = jax.nn.relu(x @ w1)
    return (hid * hid) @ w2


def setup_inputs(seed: int = 0) -> dict:
    key = jax.random.key(seed)
    ks = jax.random.split(key, 20)
    f32 = jnp.float32
    res_scale = (2.0 * DEPTH) ** -0.5
    nrm = lambda k, shape, sc: jax.random.normal(k, shape, f32) * sc
    f_bias = jnp.linspace(3.0, 6.0, MLSTM_HEADS, dtype=f32)
    gate_bias = jnp.concatenate([
        nrm(ks[9], (N_MLSTM_LAYERS, MLSTM_HEADS), 0.1),
        f_bias[None] + nrm(ks[10], (N_MLSTM_LAYERS, MLSTM_HEADS), 0.1)], axis=-1)
    return {
        "x": nrm(ks[0], (BATCH, SEQ, D_MODEL), 1.0),
        "mixer_norm": 1.0 + nrm(ks[1], (DEPTH, D_MODEL), 0.02),
        "mlp_norm": 1.0 + nrm(ks[2], (DEPTH, D_MODEL), 0.02),
        "rel_bias": nrm(ks[3], (REL_BUCKETS, N_GROUPS, ATTN_HEADS), 0.5),
        "attn_w_in": nrm(ks[4], (N_ATTN_LAYERS, D_MODEL, ATTN_IN_WIDTH), D_MODEL ** -0.5),
        "attn_q_gain": 1.0 + nrm(ks[5], (N_ATTN_LAYERS, N_GROUPS, ATTN_HEAD_DIM), 0.02),
        "attn_k_gain": 1.0 + nrm(ks[6], (N_ATTN_LAYERS, N_GROUPS, ATTN_HEAD_DIM), 0.02),
        "attn_w_out": nrm(ks[7], (N_ATTN_LAYERS, ATTN_GROUP_WIDTH, D_MODEL), ATTN_GROUP_WIDTH ** -0.5 * res_scale),
        "mlstm_w_in": nrm(ks[8], (N_MLSTM_LAYERS, D_MODEL, MLSTM_IN_WIDTH), D_MODEL ** -0.5),
        "mlstm_gate_bias": gate_bias,
        "mlstm_conv_w": nrm(ks[11], (N_MLSTM_LAYERS, CONV_WIDTH, MLSTM_QK_WIDTH), CONV_WIDTH ** -0.5),
        "mlstm_conv_b": nrm(ks[12], (N_MLSTM_LAYERS, MLSTM_QK_WIDTH), 0.02),
        "mlstm_out_gain": 1.0 + nrm(ks[13], (N_MLSTM_LAYERS, MLSTM_HEADS, MLSTM_V_DIM), 0.02),
        "mlstm_w_out": nrm(ks[14], (N_MLSTM_LAYERS, MLSTM_VW, D_MODEL), MLSTM_VW ** -0.5 * res_scale),
        "mlp_w_in": nrm(ks[15], (DEPTH, D_MODEL, D_FF), D_MODEL ** -0.5),
        "mlp_w_out": nrm(ks[16], (DEPTH, D_FF, D_MODEL), D_FF ** -0.5 * res_scale),
    }


def reference(x, mixer_norm, mlp_norm, rel_bias, attn_w_in, attn_q_gain, attn_k_gain, attn_w_out,
              mlstm_w_in, mlstm_gate_bias, mlstm_conv_w, mlstm_conv_b, mlstm_out_gain, mlstm_w_out,
              mlp_w_in, mlp_w_out):
    for layer in range(DEPTH):
        j = layer // N_MIXERS
        hnorm = rms_norm(x, mixer_norm[layer])
        if layer % N_MIXERS == 0:
            x = x + attention_mixer(hnorm, attn_w_in[j], attn_q_gain[j], attn_k_gain[j],
                                    attn_w_out[j], rel_bias)
        else:
            x = x + mlstm_mixer(hnorm, mlstm_w_in[j], mlstm_gate_bias[j], mlstm_conv_w[j],
                                mlstm_conv_b[j], mlstm_out_gain[j], mlstm_w_out[j])
        x = x + sq_relu_mlp(rms_norm(x, mlp_norm[layer]), mlp_w_in[layer], mlp_w_out[layer])
    return x
```

```python
import math
from functools import partial

import jax
import jax.numpy as jnp
from jax import lax
from jax.experimental import pallas as pl
from jax.experimental.pallas import tpu as pltpu

F32 = jnp.float32
BF16 = jnp.bfloat16

D_MODEL = 1024
EPS = 1e-6

ATTN_HEADS = 16
ATTN_HEAD_DIM = 64
DILATION_PATTERNS = ((128, 1), (512, 4), (2048, 16))
N_GROUPS = len(DILATION_PATTERNS)
BAND = 128
REL_BUCKETS = 32
REL_MAX_DISTANCE = 2048
MASKED = -1e30

MLSTM_HEADS = 4
MLSTM_V_DIM = 256
MLSTM_QK_DIM = 128
CONV_WIDTH = 4
MLSTM_CHUNK = 128

LANES = 128
SUBLANES = 8
MXU_DIM = 256
VMEM_LIMIT_BYTES = 56 * 1024 * 1024

ROW_TILE = 1024
COL_TILE = 1024


def _params(*semantics):
    return pltpu.CompilerParams(dimension_semantics=semantics, vmem_limit_bytes=VMEM_LIMIT_BYTES)


def _rms_rows(x, gain):
    ms = jnp.mean(x * x, axis=-1, keepdims=True)
    return x * lax.rsqrt(ms + EPS) * gain


def _sigmoid(x):
    return 1.0 / (1.0 + jnp.exp(-x))


def _log_sigmoid(x):
    return jnp.minimum(x, 0.0) - jnp.log1p(jnp.exp(-jnp.abs(x)))


def _attn_proj_kernel(x_ref, g_ref, w_ref, gain_ref, seg_ref, o_ref, h_ref, *, n_norm_tiles):
    j = pl.program_id(1)

    @pl.when(j == 0)
    def _():
        h_ref[...] = _rms_rows(x_ref[...], g_ref[...]).astype(BF16)

    acc = jnp.dot(h_ref[...], w_ref[...], preferred_element_type=F32)

    @pl.when(j < n_norm_tiles)
    def _():
        sq = (acc * acc).astype(BF16)
        ss = jnp.concatenate(
            [jnp.dot(sq[:, c:c + MXU_DIM], seg_ref[...], preferred_element_type=F32)
             for c in range(0, COL_TILE, MXU_DIM)], axis=1)
        y = acc * lax.rsqrt(ss * (1.0 / ATTN_HEAD_DIM) + EPS) * gain_ref[...]
        o_ref[...] = y.astype(o_ref.dtype)

    @pl.when(j >= n_norm_tiles)
    def _():
        o_ref[...] = acc.astype(o_ref.dtype)


def _attn_proj(x, g, w, gains, seg):
    t, d = x.shape
    n = w.shape[1]
    n_norm_tiles = 2 * N_GROUPS * ATTN_HEADS * ATTN_HEAD_DIM // COL_TILE
    return pl.pallas_call(
        partial(_attn_proj_kernel, n_norm_tiles=n_norm_tiles),
        grid=(t // ROW_TILE, n // COL_TILE),
        in_specs=[
            pl.BlockSpec((ROW_TILE, d), lambda i, j: (i, 0)),
            pl.BlockSpec((1, d), lambda i, j: (0, 0)),
            pl.BlockSpec((d, COL_TILE), lambda i, j: (0, j)),
            pl.BlockSpec((1, COL_TILE), lambda i, j: (0, j)),
            pl.BlockSpec((MXU_DIM, MXU_DIM), lambda i, j: (0, 0)),
        ],
        out_specs=pl.BlockSpec((ROW_TILE, COL_TILE), lambda i, j: (i, j)),
        out_shape=jax.ShapeDtypeStruct((t, n), BF16),
        scratch_shapes=[pltpu.VMEM((ROW_TILE, d), BF16)],
        compiler_params=_params("parallel", "arbitrary"),
        name="attn_proj",
    )(x, g, w, gains, seg)


def _attn_kernel(q_ref, k_ref, v_ref, bias_ref, o_ref, lse_ref, kprev_ref, vprev_ref):
    n = pl.program_id(2)

    @pl.when(n == 0)
    def _():
        kprev_ref[...] = jnp.zeros_like(kprev_ref)
        vprev_ref[...] = jnp.zeros_like(vprev_ref)

    heads_per_pass = MXU_DIM // ATTN_HEAD_DIM
    lane_head = lax.shift_right_logical(
        lax.broadcasted_iota(jnp.int32, (BAND, MXU_DIM), 1), int(math.log2(ATTN_HEAD_DIM)))
    key_is_prev = lax.broadcasted_iota(jnp.int32, (BAND, 2 * BAND), 1) < BAND
    prev_masked = jnp.logical_and(key_is_prev, n == 0)
    lse_lane = lax.broadcasted_iota(jnp.int32, (BAND, LANES), 1)
    lse_tile = jnp.zeros((BAND, LANES), F32)

    for c in range(ATTN_HEADS // heads_per_pass):
        cols = slice(c * MXU_DIM, (c + 1) * MXU_DIM)
        qq = q_ref[0, :, cols]
        kk = jnp.concatenate([kprev_ref[:, cols], k_ref[0, :, cols]], axis=0)
        vv = jnp.concatenate([vprev_ref[:, cols], v_ref[0, :, cols]], axis=0)
        lhs = jnp.concatenate(
            [jnp.where(lane_head == hh, qq, jnp.zeros_like(qq)) for hh in range(heads_per_pass)], axis=0)
        s_all = lax.dot_general(lhs, kk, (((1,), (1,)), ((), ())), preferred_element_type=F32)
        probs, inv_dens = [], []
        for hh in range(heads_per_pass):
            h = c * heads_per_pass + hh
            s = s_all[hh * BAND:(hh + 1) * BAND] + bias_ref[h]
            s = jnp.where(prev_masked, MASKED, s)
            m = jnp.max(s, axis=-1, keepdims=True)
            p = jnp.exp(s - m)
            den = jnp.sum(p, axis=-1, keepdims=True)
            probs.append(p.astype(BF16))
            inv_dens.append(1.0 / den)
            lse_tile = lse_tile + jnp.where(lse_lane == h, m + jnp.log(den), 0.0)
        o_all = jnp.dot(jnp.concatenate(probs, axis=0), vv, preferred_element_type=F32)
        o = jnp.zeros((BAND, MXU_DIM), F32)
        for hh in range(heads_per_pass):
            o = o + jnp.where(lane_head == hh, o_all[hh * BAND:(hh + 1) * BAND] * inv_dens[hh], 0.0)
        o_ref[0, :, cols] = o.astype(o_ref.dtype)

    lse_ref[0] = lse_tile
    kprev_ref[...] = k_ref[0]
    vprev_ref[...] = v_ref[0]


def _attention_group(qkv, bias, group, dilation):
    b, s, width = qkv.shape
    rows = s // dilation
    gw = ATTN_HEADS * ATTN_HEAD_DIM
    blocks_per_token = width // gw
    view = qkv.reshape(b, rows, dilation * width)

    def col_spec(which):
        return pl.BlockSpec(
            (1, BAND, gw),
            lambda bi, r, n: (bi, n, r * blocks_per_token + which * N_GROUPS + group))

    out, lse = pl.pallas_call(
        _attn_kernel,
        grid=(b, dilation, rows // BAND),
        in_specs=[col_spec(0), col_spec(1), col_spec(2),
                  pl.BlockSpec((ATTN_HEADS, BAND, 2 * BAND), lambda bi, r, n: (0, 0, 0))],
        out_specs=[pl.BlockSpec((1, BAND, gw), lambda bi, r, n: (bi, n, r)),
                   pl.BlockSpec((1, BAND, LANES), lambda bi, r, n: (bi, n, r))],
        out_shape=[jax.ShapeDtypeStruct((b, rows, dilation * gw), BF16),
                   jax.ShapeDtypeStruct((b, rows, dilation * LANES), F32)],
        scratch_shapes=[pltpu.VMEM((BAND, gw), BF16), pltpu.VMEM((BAND, gw), BF16)],
        compiler_params=_params("arbitrary", "arbitrary", "arbitrary"),
        name=f"attn_group{group}",
    )(view, view, view, bias)
    return out.reshape(b * s, gw), lse.reshape(b * s, LANES)


def _t5_bucket(dist):
    max_exact = REL_BUCKETS // 2
    d = jnp.maximum(dist.astype(F32), 1.0)
    large = max_exact + (jnp.log(d / max_exact) / math.log(REL_MAX_DISTANCE / max_exact)
                         * (REL_BUCKETS - max_exact)).astype(jnp.int32)
    large = jnp.minimum(large, REL_BUCKETS - 1)
    return jnp.where(dist < max_exact, dist, large)


def _bias_table(rel_bias_g, dilation):
    qi = jnp.arange(BAND)[:, None]
    ki = jnp.arange(2 * BAND)[None, :]
    steps = qi - ki + BAND
    valid = (steps >= 0) & (steps <= BAND)
    bias = rel_bias_g[_t5_bucket(jnp.clip(steps, 0, BAND) * dilation)]
    bias = jnp.where(valid[:, :, None], bias.astype(F32), MASKED)
    return jnp.transpose(bias, (2, 0, 1))


def _attn_out_kernel(x_ref, o0_ref, o1_ref, o2_ref, l0_ref, l1_ref, l2_ref, expand_ref, w_ref, out_ref):
    lses = [l0_ref[...], l1_ref[...], l2_ref[...]]
    m = jnp.maximum(jnp.maximum(lses[0], lses[1]), lses[2])
    es = [jnp.exp(l - m) for l in lses]
    inv = 1.0 / (es[0] + es[1] + es[2])
    y = None
    for e, o_ref in zip(es, (o0_ref, o1_ref, o2_ref)):
        wt = e * inv
        hi = wt.astype(BF16)
        lo = (wt - hi.astype(F32)).astype(BF16)
        wide = jnp.dot(jnp.concatenate([hi, lo], axis=1), expand_ref[...], preferred_element_type=F32)
        term = wide * o_ref[...].astype(F32)
        y = term if y is None else y + term
    out_ref[...] = x_ref[...] + jnp.dot(y.astype(BF16), w_ref[...], preferred_element_type=F32)


def _attn_out(x, outs, lses, expand, w):
    t, d = x.shape
    rows = ROW_TILE // 2
    row_spec = lambda width: pl.BlockSpec((rows, width), lambda i: (i, 0))
    const_spec = lambda shape: pl.BlockSpec(shape, lambda i: (0, 0))
    return pl.pallas_call(
        _attn_out_kernel,
        grid=(t // rows,),
        in_specs=[row_spec(d)] + [row_spec(d)] * 3 + [row_spec(LANES)] * 3
                 + [const_spec(expand.shape), const_spec(w.shape)],
        out_specs=row_spec(d),
        out_shape=jax.ShapeDtypeStruct((t, d), F32),
        compiler_params=_params("parallel"),
        name="attn_out",
    )(x, *outs, *lses, expand, w)


def _mlstm_proj_kernel(x_ref, g_ref, w_ref, wg_ref, gb_ref, o_ref, gates_ref, h_ref):
    j = pl.program_id(1)

    @pl.when(j == 0)
    def _():
        h = _rms_rows(x_ref[...], g_ref[...]).astype(BF16)
        h_ref[...] = h
        gates_ref[...] = jnp.dot(h, wg_ref[...], preferred_element_type=F32) + gb_ref[...]

    o_ref[...] = jnp.dot(h_ref[...], w_ref[...], preferred_element_type=F32).astype(o_ref.dtype)


def _mlstm_proj(x, g, w, wg, gb):
    t, d = x.shape
    n = w.shape[1]
    return pl.pallas_call(
        _mlstm_proj_kernel,
        grid=(t // ROW_TILE, n // COL_TILE),
        in_specs=[
            pl.BlockSpec((ROW_TILE, d), lambda i, j: (i, 0)),
            pl.BlockSpec((1, d), lambda i, j: (0, 0)),
            pl.BlockSpec((d, COL_TILE), lambda i, j: (0, j)),
            pl.BlockSpec((d, LANES), lambda i, j: (0, 0)),
            pl.BlockSpec((1, LANES), lambda i, j: (0, 0)),
        ],
        out_specs=[pl.BlockSpec((ROW_TILE, COL_TILE), lambda i, j: (i, j)),
                   pl.BlockSpec((ROW_TILE, LANES), lambda i, j: (i, 0))],
        out_shape=[jax.ShapeDtypeStruct((t, n), BF16), jax.ShapeDtypeStruct((t, LANES), F32)],
        scratch_shapes=[pltpu.VMEM((ROW_TILE, d), BF16)],
        compiler_params=_params("parallel", "arbitrary"),
        name="mlstm_proj",
    )(x, g, w, wg, gb)


def _mlstm_kernel(qk_ref, v_ref, op_ref, gates_ref, cw_ref, cb_ref, og_ref, y_ref,
                  c_scr, n_scr, m_scr, tail_scr):
    chunk = pl.program_id(1)
    L = MLSTM_CHUNK

    @pl.when(chunk == 0)
    def _():
        c_scr[...] = jnp.zeros_like(c_scr)
        n_scr[...] = jnp.zeros_like(n_scr)
        m_scr[...] = jnp.zeros_like(m_scr)
        tail_scr[...] = jnp.zeros_like(tail_scr)

    xqk = qk_ref[...].astype(F32)
    ext = jnp.concatenate([tail_scr[...], xqk], axis=0)
    cw = cw_ref[...]
    conv = xqk * cw[CONV_WIDTH - 1:CONV_WIDTH] + cb_ref[...]
    for back in range(1, CONV_WIDTH):
        shifted = pltpu.roll(ext, back, axis=0)[SUBLANES:]
        conv = conv + shifted * cw[CONV_WIDTH - 1 - back:CONV_WIDTH - back]
    tail_scr[...] = xqk[L - SUBLANES:]
    act = conv * _sigmoid(conv)

    gates = gates_ref[...]
    log_f = _log_sigmoid(gates)
    gates_t = gates.T
    log_f_t = log_f.T
    lane_id = lax.broadcasted_iota(jnp.int32, (L, LANES), 1)
    row_id = lax.broadcasted_iota(jnp.int32, (LANES, L), 0)
    t_id = lax.broadcasted_iota(jnp.int32, (L, L), 0)
    s_id = lax.broadcasted_iota(jnp.int32, (L, L), 1)
    causal = s_id <= t_id
    qk_width = MLSTM_HEADS * MLSTM_QK_DIM

    for h in range(MLSTM_HEADS):
        q = act[:, h * MLSTM_QK_DIM:(h + 1) * MLSTM_QK_DIM] * (MLSTM_QK_DIM ** -0.5)
        k = act[:, qk_width + h * MLSTM_QK_DIM:qk_width + (h + 1) * MLSTM_QK_DIM]
        v = v_ref[:, h * MLSTM_V_DIM:(h + 1) * MLSTM_V_DIM]
        i_col = jnp.sum(jnp.where(lane_id == h, gates, 0.0), axis=1, keepdims=True)
        f_col = jnp.sum(jnp.where(lane_id == MLSTM_HEADS + h, log_f, 0.0), axis=1, keepdims=True)
        i_row = jnp.sum(jnp.where(row_id == h, gates_t, 0.0), axis=0, keepdims=True)
        f_row = jnp.sum(jnp.where(row_id == MLSTM_HEADS + h, log_f_t, 0.0), axis=0, keepdims=True)
        bcum_col = jnp.sum(jnp.where(causal, f_row, 0.0), axis=1, keepdims=True)
        bcum_row = jnp.sum(jnp.where(t_id <= s_id, f_col, 0.0), axis=0, keepdims=True)
        b_last = jnp.sum(f_col, axis=0, keepdims=True)
        m_prev = m_scr[h][0:1, 0:1]
        n_prev = n_scr[h][0:1, :]
        c_prev = c_scr[h]

        dmat = jnp.where(causal, bcum_col - bcum_row + i_row, -jnp.inf)
        m_t = jnp.maximum(bcum_col + m_prev, jnp.max(dmat, axis=1, keepdims=True))
        inter = jnp.exp(bcum_col + m_prev - m_t)
        qb = q.astype(BF16)
        kb = k.astype(BF16)
        scores = lax.dot_general(qb, kb, (((1,), (1,)), ((), ())), preferred_element_type=F32)
        wts = scores * jnp.exp(dmat - m_t)
        num = (inter * jnp.dot(qb, c_prev.astype(BF16), preferred_element_type=F32)
               + jnp.dot(wts.astype(BF16), v, preferred_element_type=F32))
        den = (inter * jnp.sum(q * n_prev, axis=1, keepdims=True)
               + jnp.sum(wts, axis=1, keepdims=True))
        hc = num / jnp.maximum(jnp.abs(den), jnp.exp(-m_t))

        a_col = b_last - bcum_col + i_col
        m_new = jnp.maximum(b_last + m_prev, jnp.max(a_col, axis=0, keepdims=True))
        decay = jnp.exp(b_last + m_prev - m_new)
        wk = jnp.exp(a_col - m_new)
        wv = (wk * v.astype(F32)).astype(BF16)
        c_scr[h] = decay * c_prev + jnp.dot(k.T.astype(BF16), wv, preferred_element_type=F32)
        n_new = decay * n_prev + jnp.sum(wk * k, axis=0, keepdims=True)
        n_scr[h] = jnp.broadcast_to(n_new, (SUBLANES, MLSTM_QK_DIM))
        m_scr[h] = jnp.broadcast_to(m_new, (SUBLANES, LANES))

        vcols = slice(h * MLSTM_V_DIM, (h + 1) * MLSTM_V_DIM)
        normed = _rms_rows(hc, og_ref[:, vcols])
        y_ref[:, vcols] = (normed * _sigmoid(op_ref[:, vcols].astype(F32))).astype(y_ref.dtype)


def _mlstm_cell(proj, gates, conv_w, conv_b, out_gain, batch, seq):
    t = proj.shape[0]
    L = MLSTM_CHUNK
    chunks = seq // L
    d = D_MODEL
    col_spec = lambda which: pl.BlockSpec((L, d), lambda b, c: (b * chunks + c, which))
    const_spec = lambda shape: pl.BlockSpec(shape, lambda b, c: (0, 0))
    return pl.pallas_call(
        _mlstm_kernel,
        grid=(batch, chunks),
        in_specs=[col_spec(0), col_spec(1), col_spec(2),
                  pl.BlockSpec((L, LANES), lambda b, c: (b * chunks + c, 0)),
                  const_spec(conv_w.shape), const_spec(conv_b.shape), const_spec(out_gain.shape)],
        out_specs=pl.BlockSpec((L, d), lambda b, c: (b * chunks + c, 0)),
        out_shape=jax.ShapeDtypeStruct((t, d), BF16),
        scratch_shapes=[pltpu.VMEM((MLSTM_HEADS, MLSTM_QK_DIM, MLSTM_V_DIM), F32),
                        pltpu.VMEM((MLSTM_HEADS, SUBLANES, MLSTM_QK_DIM), F32),
                        pltpu.VMEM((MLSTM_HEADS, SUBLANES, LANES), F32),
                        pltpu.VMEM((SUBLANES, d), F32)],
        compiler_params=_params("arbitrary", "arbitrary"),
        name="mlstm_cell",
    )(proj, proj, proj, gates, conv_w, conv_b, out_gain)


def _residual_matmul_kernel(x_ref, y_ref, w_ref, out_ref):
    out_ref[...] = x_ref[...] + jnp.dot(y_ref[...], w_ref[...], preferred_element_type=F32)


def _residual_matmul(x, y, w):
    t, d = x.shape
    row_spec = pl.BlockSpec((ROW_TILE, d), lambda i: (i, 0))
    return pl.pallas_call(
        _residual_matmul_kernel,
        grid=(t // ROW_TILE,),
        in_specs=[row_spec, row_spec, pl.BlockSpec(w.shape, lambda i: (0, 0))],
        out_specs=row_spec,
        out_shape=jax.ShapeDtypeStruct((t, d), F32),
        compiler_params=_params("parallel"),
        name="residual_matmul",
    )(x, y, w)


def _mlp_kernel(x_ref, g_ref, w1_ref, w2_ref, out_ref, h_ref):
    j = pl.program_id(1)

    @pl.when(j == 0)
    def _():
        x = x_ref[...]
        h_ref[...] = _rms_rows(x, g_ref[...]).astype(BF16)
        out_ref[...] = x

    hid = jnp.maximum(jnp.dot(h_ref[...], w1_ref[...], preferred_element_type=F32), 0.0)
    out_ref[...] += jnp.dot((hid * hid).astype(BF16), w2_ref[...], preferred_element_type=F32)


def _mlp(x, g, w1, w2):
    t, d = x.shape
    f = w1.shape[1]
    return pl.pallas_call(
        _mlp_kernel,
        grid=(t // ROW_TILE, f // COL_TILE),
        in_specs=[
            pl.BlockSpec((ROW_TILE, d), lambda i, j: (i, 0)),
            pl.BlockSpec((1, d), lambda i, j: (0, 0)),
            pl.BlockSpec((d, COL_TILE), lambda i, j: (0, j)),
            pl.BlockSpec((COL_TILE, d), lambda i, j: (j, 0)),
        ],
        out_specs=pl.BlockSpec((ROW_TILE, d), lambda i, j: (i, 0)),
        out_shape=jax.ShapeDtypeStruct((t, d), F32),
        scratch_shapes=[pltpu.VMEM((ROW_TILE, d), BF16)],
        compiler_params=_params("parallel", "arbitrary"),
        name="mlp",
    )(x, g, w1, w2)


def _attention_layer(x, batch, seq, norm_g, w_in, q_gain, k_gain, w_out, rel_bias):
    gw = ATTN_HEADS * ATTN_HEAD_DIM
    gains = jnp.concatenate([
        jnp.tile(q_gain.astype(F32) * (ATTN_HEAD_DIM ** -0.5), (1, ATTN_HEADS)).reshape(-1),
        jnp.tile(k_gain.astype(F32), (1, ATTN_HEADS)).reshape(-1),
        jnp.ones((N_GROUPS * gw,), F32)])[None, :]
    head_of_lane = jnp.arange(MXU_DIM) // ATTN_HEAD_DIM
    seg = (head_of_lane[:, None] == head_of_lane[None, :]).astype(BF16)
    qkv = _attn_proj(x, norm_g[None, :], w_in.astype(BF16), gains, seg)
    qkv = qkv.reshape(batch, seq, -1)

    outs, lses = [], []
    for g, (_, dilation) in enumerate(DILATION_PATTERNS):
        o, l = _attention_group(qkv, _bias_table(rel_bias[:, g], dilation), g, dilation)
        outs.append(o)
        lses.append(l)

    head_rows = jnp.arange(LANES)[:, None]
    head_cols = (jnp.arange(gw) // ATTN_HEAD_DIM)[None, :]
    spread = (head_rows == head_cols).astype(BF16)
    expand = jnp.concatenate([spread, spread], axis=0)
    return _attn_out(x, outs, lses, expand, w_out.astype(BF16))


def _mlstm_layer(x, batch, seq, norm_g, w_in, gate_bias, conv_w, conv_b, out_gain, w_out):
    main = 2 * MLSTM_HEADS * MLSTM_QK_DIM + MLSTM_HEADS * MLSTM_V_DIM + D_MODEL
    n_gates = 2 * MLSTM_HEADS
    wg = jnp.pad(w_in[:, main:], ((0, 0), (0, LANES - n_gates))).astype(BF16)
    gb = jnp.pad(gate_bias.astype(F32), (0, LANES - n_gates))[None, :]
    proj, gates = _mlstm_proj(x, norm_g[None, :], w_in[:, :main].astype(BF16), wg, gb)
    y = _mlstm_cell(proj, gates, conv_w.astype(F32), conv_b.astype(F32)[None, :],
                    out_gain.astype(F32).reshape(1, -1), batch, seq)
    return _residual_matmul(x, y, w_out.astype(BF16))


def kernel(x, mixer_norm, mlp_norm, rel_bias, attn_w_in, attn_q_gain, attn_k_gain, attn_w_out,
           mlstm_w_in, mlstm_gate_bias, mlstm_conv_w, mlstm_conv_b, mlstm_out_gain, mlstm_w_out,
           mlp_w_in, mlp_w_out):
    batch, seq, d = x.shape
    depth = mixer_norm.shape[0]
    h = x.reshape(batch * seq, d)
    for layer in range(depth):
        j = layer // 2
        if layer % 2 == 0:
            h = _attention_layer(h, batch, seq, mixer_norm[layer], attn_w_in[j], attn_q_gain[j],
                                 attn_k_gain[j], attn_w_out[j], rel_bias)
        else:
            h = _mlstm_layer(h, batch, seq, mixer_norm[layer], mlstm_w_in[j], mlstm_gate_bias[j],
                             mlstm_conv_w[j], mlstm_conv_b[j], mlstm_out_gain[j], mlstm_w_out[j])
        h = _mlp(h, mlp_norm[layer][None, :], mlp_w_in[layer].astype(BF16), mlp_w_out[layer].astype(BF16))
    return h.reshape(batch, seq, d)
```

```python
import math
from functools import partial

import jax
import jax.numpy as jnp
from jax import lax
from jax.experimental import pallas as pl
from jax.experimental.pallas import tpu as pltpu

F32 = jnp.float32
BF16 = jnp.bfloat16

D_MODEL = 1024
EPS = 1e-6

ATTN_HEADS = 16
ATTN_HEAD_DIM = 64
DILATION_PATTERNS = ((128, 1), (512, 4), (2048, 16))
N_GROUPS = len(DILATION_PATTERNS)
BAND = 128
REL_BUCKETS = 32
REL_MAX_DISTANCE = 2048
MASKED = -1e30

MLSTM_HEADS = 4
MLSTM_V_DIM = 256
MLSTM_QK_DIM = 128
CONV_WIDTH = 4
MLSTM_CHUNK = 128

LANES = 128
SUBLANES = 8
MXU_DIM = 256
VMEM_LIMIT_BYTES = 56 * 1024 * 1024

ROW_TILE = 1024
COL_TILE = 1024


def _params(*semantics):
    return pltpu.CompilerParams(dimension_semantics=semantics, vmem_limit_bytes=VMEM_LIMIT_BYTES)


def _rms_rows(x, gain):
    ms = jnp.mean(x * x, axis=-1, keepdims=True)
    return x * lax.rsqrt(ms + EPS) * gain


def _sigmoid(x):
    return 1.0 / (1.0 + jnp.exp(-x))


def _log_sigmoid(x):
    return jnp.minimum(x, 0.0) - jnp.log1p(jnp.exp(-jnp.abs(x)))


def _attn_proj_kernel(x_ref, g_ref, w_ref, gain_ref, seg_ref, o0_ref, o1_ref, o2_ref, xn_ref, h_ref,
                      *, n_norm_tiles):
    j = pl.program_id(1)
    group = j % N_GROUPS
    out_refs = (o0_ref, o1_ref, o2_ref)

    @pl.when(j == 0)
    def _():
        xn = _rms_rows(x_ref[...], g_ref[...])
        for c in range(D_MODEL // LANES):
            xn_ref[c] = xn[:, c * LANES:(c + 1) * LANES]
        for g, (_, dil) in enumerate(DILATION_PATTERNS):
            if dil == 1:
                h_ref[g] = xn.astype(BF16)
                continue
            rows = ROW_TILE // dil
            for r in range(dil):
                for c in range(D_MODEL // LANES):
                    h_ref[g, r * rows:(r + 1) * rows, c * LANES:(c + 1) * LANES] = (
                        xn_ref[c, pl.ds(r, rows, stride=dil), :].astype(BF16))

    acc = jnp.dot(h_ref[group], w_ref[...], preferred_element_type=F32)

    def store(y):
        for g, (_, dil) in enumerate(DILATION_PATTERNS):
            @pl.when(group == g)
            def _():
                out_refs[g][0] = y.reshape(dil, ROW_TILE // dil, COL_TILE)

    @pl.when(j < n_norm_tiles)
    def _():
        sq = (acc * acc).astype(BF16)
        ss = jnp.concatenate(
            [jnp.dot(sq[:, c:c + MXU_DIM], seg_ref[...], preferred_element_type=F32)
             for c in range(0, COL_TILE, MXU_DIM)], axis=1)
        store((acc * lax.rsqrt(ss * (1.0 / ATTN_HEAD_DIM) + EPS) * gain_ref[...]).astype(BF16))

    @pl.when(j >= n_norm_tiles)
    def _():
        store(acc.astype(BF16))


def _attn_proj(x, g, w, gains, seg, batch, seq):
    t, d = x.shape
    n = w.shape[1]
    gw = ATTN_HEADS * ATTN_HEAD_DIM
    n_norm_tiles = 2 * N_GROUPS * gw // COL_TILE
    tiles_per_seq = seq // ROW_TILE

    def out_spec(dil):
        return pl.BlockSpec((1, dil, ROW_TILE // dil, COL_TILE),
                            lambda i, j: (i // tiles_per_seq, 0, i % tiles_per_seq, j // N_GROUPS))

    return pl.pallas_call(
        partial(_attn_proj_kernel, n_norm_tiles=n_norm_tiles),
        grid=(t // ROW_TILE, n // COL_TILE),
        in_specs=[
            pl.BlockSpec((ROW_TILE, d), lambda i, j: (i, 0)),
            pl.BlockSpec((1, d), lambda i, j: (0, 0)),
            pl.BlockSpec((d, COL_TILE), lambda i, j: (0, j)),
            pl.BlockSpec((1, COL_TILE), lambda i, j: (0, j)),
            pl.BlockSpec((MXU_DIM, MXU_DIM), lambda i, j: (0, 0)),
        ],
        out_specs=[out_spec(dil) for _, dil in DILATION_PATTERNS],
        out_shape=[jax.ShapeDtypeStruct((batch, dil, seq // dil, 3 * gw), BF16)
                   for _, dil in DILATION_PATTERNS],
        scratch_shapes=[pltpu.VMEM((d // LANES, ROW_TILE, LANES), F32),
                        pltpu.VMEM((N_GROUPS, ROW_TILE, d), BF16)],
        compiler_params=_params("parallel", "arbitrary"),
        name="attn_proj",
    )(x, g, w, gains, seg)


def _attn_kernel(q_ref, k_ref, v_ref, bias_ref, o_ref, lse_ref, kprev_ref, vprev_ref):
    n = pl.program_id(2)

    @pl.when(n == 0)
    def _():
        kprev_ref[...] = jnp.zeros_like(kprev_ref)
        vprev_ref[...] = jnp.zeros_like(vprev_ref)

    heads_per_pass = MXU_DIM // ATTN_HEAD_DIM
    lane_head = lax.shift_right_logical(
        lax.broadcasted_iota(jnp.int32, (BAND, MXU_DIM), 1), int(math.log2(ATTN_HEAD_DIM)))
    key_is_prev = lax.broadcasted_iota(jnp.int32, (BAND, 2 * BAND), 1) < BAND
    prev_masked = jnp.logical_and(key_is_prev, n == 0)
    lse_lane = lax.broadcasted_iota(jnp.int32, (BAND, LANES), 1)
    lse_tile = jnp.zeros((BAND, LANES), F32)

    for c in range(ATTN_HEADS // heads_per_pass):
        cols = slice(c * MXU_DIM, (c + 1) * MXU_DIM)
        qq = q_ref[0, 0, :, cols]
        kk = jnp.concatenate([kprev_ref[:, cols], k_ref[0, 0, :, cols]], axis=0)
        vv = jnp.concatenate([vprev_ref[:, cols], v_ref[0, 0, :, cols]], axis=0)
        lhs = jnp.concatenate(
            [jnp.where(lane_head == hh, qq, jnp.zeros_like(qq)) for hh in range(heads_per_pass)], axis=0)
        s_all = lax.dot_general(lhs, kk, (((1,), (1,)), ((), ())), preferred_element_type=F32)
        probs, inv_dens = [], []
        for hh in range(heads_per_pass):
            h = c * heads_per_pass + hh
            s = s_all[hh * BAND:(hh + 1) * BAND] + bias_ref[h]
            s = jnp.where(prev_masked, MASKED, s)
            m = jnp.max(s, axis=-1, keepdims=True)
            p = jnp.exp(s - m)
            den = jnp.sum(p, axis=-1, keepdims=True)
            probs.append(p.astype(BF16))
            inv_dens.append(1.0 / den)
            lse_tile = lse_tile + jnp.where(lse_lane == h, m + jnp.log(den), 0.0)
        o_all = jnp.dot(jnp.concatenate(probs, axis=0), vv, preferred_element_type=F32)
        o = jnp.zeros((BAND, MXU_DIM), F32)
        for hh in range(heads_per_pass):
            o = o + jnp.where(lane_head == hh, o_all[hh * BAND:(hh + 1) * BAND] * inv_dens[hh], 0.0)
        o_ref[0, 0, :, cols] = o.astype(o_ref.dtype)

    lse_ref[0, 0] = lse_tile
    kprev_ref[...] = k_ref[0, 0]
    vprev_ref[...] = v_ref[0, 0]


def _attention_group(qkv, bias, group):
    b, dilation, rows, _ = qkv.shape
    gw = ATTN_HEADS * ATTN_HEAD_DIM
    col_spec = lambda which: pl.BlockSpec((1, 1, BAND, gw), lambda bi, r, n: (bi, r, n, which))
    return pl.pallas_call(
        _attn_kernel,
        grid=(b, dilation, rows // BAND),
        in_specs=[col_spec(0), col_spec(1), col_spec(2),
                  pl.BlockSpec((ATTN_HEADS, BAND, 2 * BAND), lambda bi, r, n: (0, 0, 0))],
        out_specs=[pl.BlockSpec((1, 1, BAND, gw), lambda bi, r, n: (bi, r, n, 0)),
                   pl.BlockSpec((1, 1, BAND, LANES), lambda bi, r, n: (bi, r, n, 0))],
        out_shape=[jax.ShapeDtypeStruct((b, dilation, rows, gw), BF16),
                   jax.ShapeDtypeStruct((b, dilation, rows, LANES), F32)],
        scratch_shapes=[pltpu.VMEM((BAND, gw), BF16), pltpu.VMEM((BAND, gw), BF16)],
        compiler_params=_params("arbitrary", "arbitrary", "arbitrary"),
        name=f"attn_group{group}",
    )(qkv, qkv, qkv, bias)


def _t5_bucket(dist):
    max_exact = REL_BUCKETS // 2
    d = jnp.maximum(dist.astype(F32), 1.0)
    large = max_exact + (jnp.log(d / max_exact) / math.log(REL_MAX_DISTANCE / max_exact)
                         * (REL_BUCKETS - max_exact)).astype(jnp.int32)
    large = jnp.minimum(large, REL_BUCKETS - 1)
    return jnp.where(dist < max_exact, dist, large)


def _bias_table(rel_bias_g, dilation):
    qi = jnp.arange(BAND)[:, None]
    ki = jnp.arange(2 * BAND)[None, :]
    steps = qi - ki + BAND
    valid = (steps >= 0) & (steps <= BAND)
    bucket = _t5_bucket(jnp.clip(steps, 0, BAND) * dilation)
    onehot = bucket[None] == jnp.arange(REL_BUCKETS)[:, None, None]
    bias = jnp.sum(jnp.where(onehot[:, None], rel_bias_g.astype(F32)[:, :, None, None], 0.0), axis=0)
    return jnp.where(valid[None], bias, MASKED)


ATTN_OUT_ROWS = 512


def _attn_out_kernel(x_ref, o0_ref, o1_ref, o2_ref, l0_ref, l1_ref, l2_ref, expand_ref, w_ref, out_ref,
                     o_scr, l_scr):
    n_slabs = D_MODEL // LANES
    lses, outs = [], []
    for g, (o_ref, l_ref) in enumerate(((o0_ref, l0_ref), (o1_ref, l1_ref), (o2_ref, l2_ref))):
        dil = DILATION_PATTERNS[g][1]
        if dil == 1:
            outs.append(o_ref[0, 0].astype(F32))
            lses.append(l_ref[0, 0])
            continue
        rows = ATTN_OUT_ROWS // dil
        for r in range(dil):
            o_res = o_ref[0, r].astype(F32)
            for c in range(n_slabs):
                o_scr[g, c, pl.ds(r, rows, stride=dil), :] = o_res[:, c * LANES:(c + 1) * LANES]
            l_scr[g, pl.ds(r, rows, stride=dil), :] = l_ref[0, r]
        outs.append(jnp.concatenate([o_scr[g, c] for c in range(n_slabs)], axis=1))
        lses.append(l_scr[g])
    m = jnp.maximum(jnp.maximum(lses[0], lses[1]), lses[2])
    es = [jnp.exp(l - m) for l in lses]
    inv = 1.0 / (es[0] + es[1] + es[2])
    y = None
    for g, e in enumerate(es):
        wt = e * inv
        hi = wt.astype(BF16)
        lo = (wt - hi.astype(F32)).astype(BF16)
        wide = jnp.dot(jnp.concatenate([hi, lo], axis=1), expand_ref[...], preferred_element_type=F32)
        term = wide * outs[g]
        y = term if y is None else y + term
    out_ref[...] = x_ref[...] + jnp.dot(y.astype(BF16), w_ref[...], preferred_element_type=F32)


def _attn_out(x, outs, lses, expand, w, seq):
    t, d = x.shape
    rows = ATTN_OUT_ROWS
    tiles_per_seq = seq // rows
    row_spec = pl.BlockSpec((rows, d), lambda i: (i, 0))
    const_spec = lambda shape: pl.BlockSpec(shape, lambda i: (0, 0))

    def group_spec(dil, width):
        return pl.BlockSpec((1, dil, rows // dil, width),
                            lambda i: (i // tiles_per_seq, 0, i % tiles_per_seq, 0))

    return pl.pallas_call(
        _attn_out_kernel,
        grid=(t // rows,),
        in_specs=[row_spec]
                 + [group_spec(dil, d) for _, dil in DILATION_PATTERNS]
                 + [group_spec(dil, LANES) for _, dil in DILATION_PATTERNS]
                 + [const_spec(expand.shape), const_spec(w.shape)],
        out_specs=row_spec,
        out_shape=jax.ShapeDtypeStruct((t, d), F32),
        scratch_shapes=[pltpu.VMEM((N_GROUPS, d // LANES, rows, LANES), F32),
                        pltpu.VMEM((N_GROUPS, rows, LANES), F32)],
        compiler_params=_params("parallel"),
        name="attn_out",
    )(x, *outs, *lses, expand, w)


def _mlstm_proj_kernel(x_ref, g_ref, w_ref, wg_ref, gb_ref, o_ref, gates_ref, h_ref):
    j = pl.program_id(1)

    @pl.when(j == 0)
    def _():
        h = _rms_rows(x_ref[...], g_ref[...]).astype(BF16)
        h_ref[...] = h
        gates_ref[...] = jnp.dot(h, wg_ref[...], preferred_element_type=F32) + gb_ref[...]

    o_ref[...] = jnp.dot(h_ref[...], w_ref[...], preferred_element_type=F32).astype(o_ref.dtype)


def _mlstm_proj(x, g, w, wg, gb):
    t, d = x.shape
    n = w.shape[1]
    return pl.pallas_call(
        _mlstm_proj_kernel,
        grid=(t // ROW_TILE, n // COL_TILE),
        in_specs=[
            pl.BlockSpec((ROW_TILE, d), lambda i, j: (i, 0)),
            pl.BlockSpec((1, d), lambda i, j: (0, 0)),
            pl.BlockSpec((d, COL_TILE), lambda i, j: (0, j)),
            pl.BlockSpec((d, LANES), lambda i, j: (0, 0)),
            pl.BlockSpec((1, LANES), lambda i, j: (0, 0)),
        ],
        out_specs=[pl.BlockSpec((ROW_TILE, COL_TILE), lambda i, j: (i, j)),
                   pl.BlockSpec((ROW_TILE, LANES), lambda i, j: (i, 0))],
        out_shape=[jax.ShapeDtypeStruct((t, n), BF16), jax.ShapeDtypeStruct((t, LANES), F32)],
        scratch_shapes=[pltpu.VMEM((ROW_TILE, d), BF16)],
        compiler_params=_params("parallel", "arbitrary"),
        name="mlstm_proj",
    )(x, g, w, wg, gb)


def _mlstm_kernel(qk_ref, v_ref, op_ref, gates_ref, cw_ref, cb_ref, og_ref, y_ref,
                  c_scr, n_scr, m_scr, tail_scr):
    chunk = pl.program_id(1)
    L = MLSTM_CHUNK

    @pl.when(chunk == 0)
    def _():
        c_scr[...] = jnp.zeros_like(c_scr)
        n_scr[...] = jnp.zeros_like(n_scr)
        m_scr[...] = jnp.zeros_like(m_scr)
        tail_scr[...] = jnp.zeros_like(tail_scr)

    xqk = qk_ref[...].astype(F32)
    ext = jnp.concatenate([tail_scr[...], xqk], axis=0)
    cw = cw_ref[...]
    conv = xqk * cw[CONV_WIDTH - 1:CONV_WIDTH] + cb_ref[...]
    for back in range(1, CONV_WIDTH):
        shifted = pltpu.roll(ext, back, axis=0)[SUBLANES:]
        conv = conv + shifted * cw[CONV_WIDTH - 1 - back:CONV_WIDTH - back]
    tail_scr[...] = xqk[L - SUBLANES:]
    act = conv * _sigmoid(conv)

    gates = gates_ref[...]
    log_f = _log_sigmoid(gates)
    gates_t = gates.T
    log_f_t = log_f.T
    lane_id = lax.broadcasted_iota(jnp.int32, (L, LANES), 1)
    row_id = lax.broadcasted_iota(jnp.int32, (LANES, L), 0)
    t_id = lax.broadcasted_iota(jnp.int32, (L, L), 0)
    s_id = lax.broadcasted_iota(jnp.int32, (L, L), 1)
    causal = s_id <= t_id
    qk_width = MLSTM_HEADS * MLSTM_QK_DIM

    for h in range(MLSTM_HEADS):
        q = act[:, h * MLSTM_QK_DIM:(h + 1) * MLSTM_QK_DIM] * (MLSTM_QK_DIM ** -0.5)
        k = act[:, qk_width + h * MLSTM_QK_DIM:qk_width + (h + 1) * MLSTM_QK_DIM]
        v = v_ref[:, h * MLSTM_V_DIM:(h + 1) * MLSTM_V_DIM]
        i_col = jnp.sum(jnp.where(lane_id == h, gates, 0.0), axis=1, keepdims=True)
        f_col = jnp.sum(jnp.where(lane_id == MLSTM_HEADS + h, log_f, 0.0), axis=1, keepdims=True)
        i_row = jnp.sum(jnp.where(row_id == h, gates_t, 0.0), axis=0, keepdims=True)
        f_row = jnp.sum(jnp.where(row_id == MLSTM_HEADS + h, log_f_t, 0.0), axis=0, keepdims=True)
        bcum_col = jnp.sum(jnp.where(causal, f_row, 0.0), axis=1, keepdims=True)
        bcum_row = jnp.sum(jnp.where(t_id <= s_id, f_col, 0.0), axis=0, keepdims=True)
        b_last = jnp.sum(f_col, axis=0, keepdims=True)
        m_prev = m_scr[h][0:1, 0:1]
        n_prev = n_scr[h][0:1, :]
        c_prev = c_scr[h]

        dmat = jnp.where(causal, bcum_col - bcum_row + i_row, -jnp.inf)
        m_t = jnp.maximum(bcum_col + m_prev, jnp.max(dmat, axis=1, keepdims=True))
        inter = jnp.exp(bcum_col + m_prev - m_t)
        qb = q.astype(BF16)
        kb = k.astype(BF16)
        scores = lax.dot_general(qb, kb, (((1,), (1,)), ((), ())), preferred_element_type=F32)
        wts = scores * jnp.exp(dmat - m_t)
        num = (inter * jnp.dot(qb, c_prev.astype(BF16), preferred_element_type=F32)
               + jnp.dot(wts.astype(BF16), v, preferred_element_type=F32))
        den = (inter * jnp.sum(q * n_prev, axis=1, keepdims=True)
               + jnp.sum(wts, axis=1, keepdims=True))
        hc = num / jnp.maximum(jnp.abs(den), jnp.exp(-m_t))

        a_col = b_last - bcum_col + i_col
        m_new = jnp.maximum(b_last + m_prev, jnp.max(a_col, axis=0, keepdims=True))
        decay = jnp.exp(b_last + m_prev - m_new)
        wk = jnp.exp(a_col - m_new)
        wv = (wk * v.astype(F32)).astype(BF16)
        c_scr[h] = decay * c_prev + jnp.dot(k.T.astype(BF16), wv, preferred_element_type=F32)
        n_new = decay * n_prev + jnp.sum(wk * k, axis=0, keepdims=True)
        n_scr[h] = jnp.broadcast_to(n_new, (SUBLANES, MLSTM_QK_DIM))
        m_scr[h] = jnp.broadcast_to(m_new, (SUBLANES, LANES))

        vcols = slice(h * MLSTM_V_DIM, (h + 1) * MLSTM_V_DIM)
        normed = _rms_rows(hc, og_ref[:, vcols])
        y_ref[:, vcols] = (normed * _sigmoid(op_ref[:, vcols].astype(F32))).astype(y_ref.dtype)


def _mlstm_cell(proj, gates, conv_w, conv_b, out_gain, batch, seq):
    t = proj.shape[0]
    L = MLSTM_CHUNK
    chunks = seq // L
    d = D_MODEL
    col_spec = lambda which: pl.BlockSpec((L, d), lambda b, c: (b * chunks + c, which))
    const_spec = lambda shape: pl.BlockSpec(shape, lambda b, c: (0, 0))
    return pl.pallas_call(
        _mlstm_kernel,
        grid=(batch, chunks),
        in_specs=[col_spec(0), col_spec(1), col_spec(2),
                  pl.BlockSpec((L, LANES), lambda b, c: (b * chunks + c, 0)),
                  const_spec(conv_w.shape), const_spec(conv_b.shape), const_spec(out_gain.shape)],
        out_specs=pl.BlockSpec((L, d), lambda b, c: (b * chunks + c, 0)),
        out_shape=jax.ShapeDtypeStruct((t, d), BF16),
        scratch_shapes=[pltpu.VMEM((MLSTM_HEADS, MLSTM_QK_DIM, MLSTM_V_DIM), F32),
                        pltpu.VMEM((MLSTM_HEADS, SUBLANES, MLSTM_QK_DIM), F32),
                        pltpu.VMEM((MLSTM_HEADS, SUBLANES, LANES), F32),
                        pltpu.VMEM((SUBLANES, d), F32)],
        compiler_params=_params("arbitrary", "arbitrary"),
        name="mlstm_cell",
    )(proj, proj, proj, gates, conv_w, conv_b, out_gain)


def _residual_matmul_kernel(x_ref, y_ref, w_ref, out_ref):
    out_ref[...] = x_ref[...] + jnp.dot(y_ref[...], w_ref[...], preferred_element_type=F32)


def _residual_matmul(x, y, w):
    t, d = x.shape
    row_spec = pl.BlockSpec((ROW_TILE, d), lambda i: (i, 0))
    return pl.pallas_call(
        _residual_matmul_kernel,
        grid=(t // ROW_TILE,),
        in_specs=[row_spec, row_spec, pl.BlockSpec(w.shape, lambda i: (0, 0))],
        out_specs=row_spec,
        out_shape=jax.ShapeDtypeStruct((t, d), F32),
        compiler_params=_params("parallel"),
        name="residual_matmul",
    )(x, y, w)


def _mlp_kernel(x_ref, g_ref, w1_ref, w2_ref, out_ref, h_ref):
    j = pl.program_id(1)

    @pl.when(j == 0)
    def _():
        x = x_ref[...]
        h_ref[...] = _rms_rows(x, g_ref[...]).astype(BF16)
        out_ref[...] = x

    hid = jnp.maximum(jnp.dot(h_ref[...], w1_ref[...], preferred_element_type=F32), 0.0)
    out_ref[...] += jnp.dot((hid * hid).astype(BF16), w2_ref[...], preferred_element_type=F32)


def _mlp(x, g, w1, w2):
    t, d = x.shape
    f = w1.shape[1]
    return pl.pallas_call(
        _mlp_kernel,
        grid=(t // ROW_TILE, f // COL_TILE),
        in_specs=[
            pl.BlockSpec((ROW_TILE, d), lambda i, j: (i, 0)),
            pl.BlockSpec((1, d), lambda i, j: (0, 0)),
            pl.BlockSpec((d, COL_TILE), lambda i, j: (0, j)),
            pl.BlockSpec((COL_TILE, d), lambda i, j: (j, 0)),
        ],
        out_specs=pl.BlockSpec((ROW_TILE, d), lambda i, j: (i, 0)),
        out_shape=jax.ShapeDtypeStruct((t, d), F32),
        scratch_shapes=[pltpu.VMEM((ROW_TILE, d), BF16)],
        compiler_params=_params("parallel", "arbitrary"),
        name="mlp",
    )(x, g, w1, w2)


def _attention_layer(x, batch, seq, norm_g, w_in, q_gain, k_gain, w_out, rel_bias):
    gw = ATTN_HEADS * ATTN_HEAD_DIM
    gains = jnp.concatenate([
        jnp.tile(q_gain.astype(F32) * (ATTN_HEAD_DIM ** -0.5), (1, ATTN_HEADS)).reshape(-1),
        jnp.tile(k_gain.astype(F32), (1, ATTN_HEADS)).reshape(-1),
        jnp.ones((N_GROUPS * gw,), F32)])[None, :]
    head_of_lane = jnp.arange(MXU_DIM) // ATTN_HEAD_DIM
    seg = (head_of_lane[:, None] == head_of_lane[None, :]).astype(BF16)
    qkvs = _attn_proj(x, norm_g[None, :], w_in.astype(BF16), gains, seg, batch, seq)

    outs, lses = [], []
    for g, (_, dilation) in enumerate(DILATION_PATTERNS):
        o, l = _attention_group(qkvs[g], _bias_table(rel_bias[:, g], dilation), g)
        outs.append(o)
        lses.append(l)

    head_rows = jnp.arange(LANES)[:, None]
    head_cols = (jnp.arange(gw) // ATTN_HEAD_DIM)[None, :]
    spread = (head_rows == head_cols).astype(BF16)
    expand = jnp.concatenate([spread, spread], axis=0)
    return _attn_out(x, outs, lses, expand, w_out.astype(BF16), seq)


def _mlstm_layer(x, batch, seq, norm_g, w_in, gate_bias, conv_w, conv_b, out_gain, w_out):
    main = 2 * MLSTM_HEADS * MLSTM_QK_DIM + MLSTM_HEADS * MLSTM_V_DIM + D_MODEL
    n_gates = 2 * MLSTM_HEADS
    wg = jnp.pad(w_in[:, main:], ((0, 0), (0, LANES - n_gates))).astype(BF16)
    gb = jnp.pad(gate_bias.astype(F32), (0, LANES - n_gates))[None, :]
    proj, gates = _mlstm_proj(x, norm_g[None, :], w_in[:, :main].astype(BF16), wg, gb)
    y = _mlstm_cell(proj, gates, conv_w.astype(F32), conv_b.astype(F32)[None, :],
                    out_gain.astype(F32).reshape(1, -1), batch, seq)
    return _residual_matmul(x, y, w_out.astype(BF16))


def kernel(x, mixer_norm, mlp_norm, rel_bias, attn_w_in, attn_q_gain, attn_k_gain, attn_w_out,
           mlstm_w_in, mlstm_gate_bias, mlstm_conv_w, mlstm_conv_b, mlstm_out_gain, mlstm_w_out,
           mlp_w_in, mlp_w_out):
    batch, seq, d = x.shape
    depth = mixer_norm.shape[0]
    h = x.reshape(batch * seq, d)
    for layer in range(depth):
        j = layer // 2
        if layer % 2 == 0:
            h = _attention_layer(h, batch, seq, mixer_norm[layer], attn_w_in[j], attn_q_gain[j],
                                 attn_k_gain[j], attn_w_out[j], rel_bias)
        else:
            h = _mlstm_layer(h, batch, seq, mixer_norm[layer], mlstm_w_in[j], mlstm_gate_bias[j],
                             mlstm_conv_w[j], mlstm_conv_b[j], mlstm_out_gain[j], mlstm_w_out[j])
        h = _mlp(h, mlp_norm[layer][None, :], mlp_w_in[layer].astype(BF16), mlp_w_out[layer].astype(BF16))
    return h.reshape(batch, seq, d)
```

```python
import math
from functools import partial

import jax
import jax.numpy as jnp
from jax import lax
from jax.experimental import pallas as pl
from jax.experimental.pallas import tpu as pltpu

F32 = jnp.float32
BF16 = jnp.bfloat16

D_MODEL = 1024
EPS = 1e-6
LOG2_E = math.log2(math.e)
LN_2 = math.log(2.0)

ATTN_HEADS = 16
ATTN_HEAD_DIM = 64
DILATION_PATTERNS = ((128, 1), (512, 4), (2048, 16))
N_GROUPS = len(DILATION_PATTERNS)
BAND = 128
REL_BUCKETS = 32
REL_MAX_DISTANCE = 2048
MASKED = -1e30
PERM_TILE = 512
ATTN_BLOCKS_PER_STEP = 4

MLSTM_HEADS = 4
MLSTM_V_DIM = 256
MLSTM_QK_DIM = 128
CONV_WIDTH = 4
MLSTM_CHUNK = 128
MLSTM_BATCH_PER_STEP = 1

LANES = 128
SUBLANES = 8
MXU_DIM = 256
VMEM_LIMIT_BYTES = 56 * 1024 * 1024

ROW_TILE = 1024
COL_TILE = 1024
N_SLABS = D_MODEL // LANES


def _params(*semantics):
    return pltpu.CompilerParams(dimension_semantics=semantics, vmem_limit_bytes=VMEM_LIMIT_BYTES)


def _rms_rows(x, gain):
    ms = jnp.mean(x * x, axis=-1, keepdims=True)
    return x * lax.rsqrt(ms + EPS) * gain


def _sigmoid(x):
    return 0.5 * jnp.tanh(0.5 * x) + 0.5


def _log_sigmoid(x):
    return jnp.minimum(x, 0.0) - jnp.log1p(jnp.exp(-jnp.abs(x)))


def _attn_proj_kernel(x_ref, g_ref, w_ref, gain_ref, eps_ref, seg_ref, o_ref,
                      xn_ref, s1_ref, h_ref, acc_a_ref, acc_b_ref, *, n_col_tiles, n_steps):
    t = pl.program_id(0)
    j = t % n_col_tiles
    d1 = DILATION_PATTERNS[1][1]
    d2 = DILATION_PATTERNS[2][1]
    assert DILATION_PATTERNS[0][1] == 1 and d2 == d1 * d1

    @pl.when(jnp.logical_and(j == 0, t < n_steps - 1))
    def _():
        xn = _rms_rows(x_ref[...], g_ref[...])
        h_ref[0] = xn.astype(BF16)
        for c in range(N_SLABS):
            xn_ref[c] = xn[:, c * LANES:(c + 1) * LANES]
        rows1 = PERM_TILE // d1
        rows2 = PERM_TILE // d2
        for base in range(0, ROW_TILE, PERM_TILE):
            for r in range(d1):
                dst = slice(base + r * rows1, base + (r + 1) * rows1)
                for c in range(N_SLABS):
                    blk = xn_ref[c, pl.ds(base + r, rows1, stride=d1), :]
                    s1_ref[c, dst, :] = blk
                    h_ref[1, dst, c * LANES:(c + 1) * LANES] = blk.astype(BF16)
            for r1 in range(d1):
                for r2 in range(d1):
                    r = r1 + d1 * r2
                    dst = slice(base + r * rows2, base + (r + 1) * rows2)
                    for c in range(N_SLABS):
                        blk = s1_ref[c, pl.ds(base + r1 * rows1 + r2, rows2, stride=d1), :]
                        h_ref[2, dst, c * LANES:(c + 1) * LANES] = blk.astype(BF16)

    @pl.when(t == 0)
    def _():
        acc_b_ref[...] = jnp.zeros_like(acc_b_ref)

    def step(acc_w_ref, acc_r_ref):
        acc_w_ref[...] = jnp.dot(h_ref[j % N_GROUPS], w_ref[...], preferred_element_type=F32)
        prev = acc_r_ref[...]
        sq = (prev * prev).astype(BF16)
        ms = jnp.concatenate(
            [jnp.dot(sq[:, c:c + MXU_DIM], seg_ref[0], preferred_element_type=F32)
             for c in range(0, COL_TILE, MXU_DIM)], axis=1)
        o_ref[...] = (prev * lax.rsqrt(ms + eps_ref[...]) * gain_ref[...]).astype(BF16)

    @pl.when(t % 2 == 0)
    def _():
        step(acc_a_ref, acc_b_ref)

    @pl.when(t % 2 == 1)
    def _():
        step(acc_b_ref, acc_a_ref)


def _attn_proj(x, g, w, gains, eps_row, seg):
    t, d = x.shape
    n = w.shape[1]
    n_col_tiles = n // COL_TILE
    n_row_tiles = t // ROW_TILE
    n_steps = n_row_tiles * n_col_tiles + 1
    n_norm_tiles = 2 * N_GROUPS * ATTN_HEADS * ATTN_HEAD_DIM // COL_TILE

    prev = lambda s: jnp.maximum(s - 1, 0)
    prev_col = lambda s: prev(s) % n_col_tiles
    return pl.pallas_call(
        partial(_attn_proj_kernel, n_col_tiles=n_col_tiles, n_steps=n_steps),
        grid=(n_steps,),
        in_specs=[
            pl.BlockSpec((ROW_TILE, d), lambda s: (jnp.minimum(s // n_col_tiles, n_row_tiles - 1), 0)),
            pl.BlockSpec((1, d), lambda s: (0, 0)),
            pl.BlockSpec((d, COL_TILE), lambda s: (0, s % n_col_tiles)),
            pl.BlockSpec((1, COL_TILE), lambda s: (0, prev_col(s))),
            pl.BlockSpec((1, COL_TILE), lambda s: (0, prev_col(s))),
            pl.BlockSpec((1, MXU_DIM, MXU_DIM), lambda s: (jnp.where(prev_col(s) < n_norm_tiles, 0, 1), 0, 0)),
        ],
        out_specs=pl.BlockSpec((ROW_TILE, COL_TILE), lambda s: (prev(s) // n_col_tiles, prev_col(s))),
        out_shape=jax.ShapeDtypeStruct((t, n), BF16),
        scratch_shapes=[pltpu.VMEM((N_SLABS, ROW_TILE, LANES), F32),
                        pltpu.VMEM((N_SLABS, ROW_TILE, LANES), F32),
                        pltpu.VMEM((N_GROUPS, ROW_TILE, d), BF16),
                        pltpu.VMEM((ROW_TILE, COL_TILE), F32),
                        pltpu.VMEM((ROW_TILE, COL_TILE), F32)],
        compiler_params=_params("arbitrary"),
        name="attn_proj",
    )(x, g, w, gains, eps_row, seg)


def _attn_kernel(q_ref, k_ref, v_ref, bias_ref, o_ref, lse_ref, kprev_ref, vprev_ref):
    n = pl.program_id(2)
    gw = ATTN_HEADS * ATTN_HEAD_DIM

    @pl.when(n == 0)
    def _():
        kprev_ref[...] = jnp.zeros_like(kprev_ref)
        vprev_ref[...] = jnp.zeros_like(vprev_ref)

    first = (n == 0).astype(jnp.int32)
    heads_per_pass = MXU_DIM // ATTN_HEAD_DIM
    lane_head = lax.shift_right_logical(
        lax.broadcasted_iota(jnp.int32, (BAND, MXU_DIM), 1), int(math.log2(ATTN_HEAD_DIM)))
    low_half = lax.broadcasted_iota(jnp.int32, (BAND, LANES), 1) < ATTN_HEAD_DIM
    lse_lane = lax.broadcasted_iota(jnp.int32, (BAND, LANES), 1)
    n_blocks = q_ref.shape[0] * q_ref.shape[1] // BAND
    step_rows = n_blocks * BAND
    lse_tiles = [jnp.zeros((BAND, LANES), F32) for _ in range(n_blocks)]

    for c in range(ATTN_HEADS // heads_per_pass):
        cols = slice(c * MXU_DIM, (c + 1) * MXU_DIM)
        q_all = q_ref[:, :, cols].reshape(step_rows, MXU_DIM)
        k_all = k_ref[:, :, cols].reshape(step_rows, MXU_DIM)
        v_all = v_ref[:, :, cols].reshape(step_rows, MXU_DIM)
        k_prev = kprev_ref[:, cols]
        v_prev = vprev_ref[:, cols]
        kprev_ref[:, cols] = k_all[step_rows - BAND:]
        vprev_ref[:, cols] = v_all[step_rows - BAND:]
        outs = []
        for blk in range(n_blocks):
            rows = slice(blk * BAND, (blk + 1) * BAND)
            qq = q_all[rows]
            kk = jnp.concatenate([k_prev, k_all[rows]], axis=0)
            vv = jnp.concatenate([v_prev, v_all[rows]], axis=0)
            k_prev, v_prev = k_all[rows], v_all[rows]
            table = first if blk == 0 else 0
            lhs = jnp.concatenate(
                [jnp.where(lane_head == hh, qq, jnp.zeros_like(qq)) for hh in range(heads_per_pass)], axis=0)
            s_all = lax.dot_general(lhs, kk, (((1,), (1,)), ((), ())), preferred_element_type=F32)
            probs, inv_dens = [], []
            for hh in range(heads_per_pass):
                h = c * heads_per_pass + hh
                s = s_all[hh * BAND:(hh + 1) * BAND] + bias_ref[table, h]
                m = jnp.max(s, axis=-1, keepdims=True)
                p = jnp.exp2(s - m)
                den = jnp.sum(p, axis=-1, keepdims=True)
                probs.append(p.astype(BF16))
                inv_dens.append(1.0 / den)
                lse_tiles[blk] = jnp.where(lse_lane == h, (m + jnp.log2(den)) * LN_2, lse_tiles[blk])
            o_all = jnp.dot(jnp.concatenate(probs, axis=0), vv, preferred_element_type=F32)
            halves = []
            for half in range(MXU_DIM // LANES):
                lanes = slice(half * LANES, (half + 1) * LANES)
                h0 = 2 * half
                a = o_all[h0 * BAND:(h0 + 1) * BAND, lanes] * inv_dens[h0]
                b = o_all[(h0 + 1) * BAND:(h0 + 2) * BAND, lanes] * inv_dens[h0 + 1]
                halves.append(jnp.where(low_half, a, b))
            outs.append(jnp.concatenate(halves, axis=1).astype(BF16))
        o_ref[:, :, cols] = jnp.concatenate(outs, axis=0).reshape(o_ref.shape[0], o_ref.shape[1], MXU_DIM)

    lse_ref[...] = jnp.concatenate(lse_tiles, axis=0).reshape(lse_ref.shape)


def _attention_group(qkv, bias, group, batch, seq):
    dilation = DILATION_PATTERNS[group][1]
    gw = ATTN_HEADS * ATTN_HEAD_DIM
    tiles_per_seq = seq // PERM_TILE
    rows_per_residue = PERM_TILE // dilation
    n_tiles = batch * tiles_per_seq

    step_rows = ATTN_BLOCKS_PER_STEP * BAND
    if rows_per_residue >= step_rows:
        per = rows_per_residue // step_rows
        lead, rows = 1, step_rows
        row_idx = lambda b, r, n: (b * tiles_per_seq + n // per, r * per + n % per)
    else:
        lead, rows = step_rows // rows_per_residue, rows_per_residue
        row_idx = lambda b, r, n: (b * (tiles_per_seq // lead) + n, r)

    def spec(width, col):
        return pl.BlockSpec((lead, rows, width), lambda b, r, n: (*row_idx(b, r, n), col))

    view = qkv.reshape(n_tiles, PERM_TILE, qkv.shape[1])
    out, lse = pl.pallas_call(
        _attn_kernel,
        grid=(batch, dilation, seq // dilation // step_rows),
        in_specs=[spec(gw, group), spec(gw, N_GROUPS + group), spec(gw, 2 * N_GROUPS + group),
                  pl.BlockSpec(bias.shape, lambda b, r, n: (0, 0, 0, 0))],
        out_specs=[spec(gw, 0), spec(LANES, 0)],
        out_shape=[jax.ShapeDtypeStruct((n_tiles, PERM_TILE, gw), BF16),
                   jax.ShapeDtypeStruct((n_tiles, PERM_TILE, LANES), F32)],
        scratch_shapes=[pltpu.VMEM((BAND, gw), BF16), pltpu.VMEM((BAND, gw), BF16)],
        compiler_params=_params("arbitrary", "arbitrary", "arbitrary"),
        name=f"attn_group{group}",
    )(view, view, view, bias)
    return out.reshape(batch * seq, gw), lse.reshape(batch * seq, LANES)


def _t5_bucket(dist):
    max_exact = REL_BUCKETS // 2
    d = jnp.maximum(dist.astype(F32), 1.0)
    large = max_exact + (jnp.log(d / max_exact) / math.log(REL_MAX_DISTANCE / max_exact)
                         * (REL_BUCKETS - max_exact)).astype(jnp.int32)
    large = jnp.minimum(large, REL_BUCKETS - 1)
    return jnp.where(dist < max_exact, dist, large)


def _bias_table(rel_bias_g, dilation):
    qi = jnp.arange(BAND)[:, None]
    ki = jnp.arange(2 * BAND)[None, :]
    steps = qi - ki + BAND
    valid = (steps >= 0) & (steps <= BAND)
    bucket = _t5_bucket(jnp.clip(steps, 0, BAND) * dilation)
    onehot = bucket[None] == jnp.arange(REL_BUCKETS)[:, None, None]
    bias = jnp.sum(jnp.where(onehot[:, None], rel_bias_g.astype(F32)[:, :, None, None], 0.0), axis=0)
    bias = bias * LOG2_E
    regular = jnp.where(valid[None], bias, MASKED)
    first = jnp.where((valid & (ki >= BAND))[None], bias, MASKED)
    return jnp.stack([regular, first])


def _attn_out_kernel(x_ref, o0_ref, o1_ref, o2_ref, l0_ref, l1_ref, l2_ref, expand_ref, w_ref, out_ref,
                     o_scr, l_scr):
    lses, outs = [], []
    for g, (o_ref, l_ref) in enumerate(((o0_ref, l0_ref), (o1_ref, l1_ref), (o2_ref, l2_ref))):
        dil = DILATION_PATTERNS[g][1]
        if dil == 1:
            outs.append(o_ref[...].astype(F32))
            lses.append(l_ref[...])
            continue
        rows = PERM_TILE // dil
        for r in range(dil):
            o_res = o_ref[r * rows:(r + 1) * rows, :].astype(F32)
            for c in range(N_SLABS):
                o_scr[g - 1, c, pl.ds(r, rows, stride=dil), :] = o_res[:, c * LANES:(c + 1) * LANES]
            l_scr[g - 1, pl.ds(r, rows, stride=dil), :] = l_ref[r * rows:(r + 1) * rows, :]
        outs.append(jnp.concatenate([o_scr[g - 1, c] for c in range(N_SLABS)], axis=1))
        lses.append(l_scr[g - 1])
    m = jnp.maximum(jnp.maximum(lses[0], lses[1]), lses[2])
    es = [jnp.exp(l - m) for l in lses]
    inv = 1.0 / (es[0] + es[1] + es[2])
    y = None
    for g, e in enumerate(es):
        wt = e * inv
        hi = wt.astype(BF16)
        lo = (wt - hi.astype(F32)).astype(BF16)
        wide = jnp.dot(jnp.concatenate([hi, lo], axis=1), expand_ref[...], preferred_element_type=F32)
        term = wide * outs[g]
        y = term if y is None else y + term
    out_ref[...] = x_ref[...] + jnp.dot(y.astype(BF16), w_ref[...], preferred_element_type=F32)


def _attn_out(x, outs, lses, expand, w):
    t, d = x.shape
    rows = PERM_TILE
    row_spec = lambda width: pl.BlockSpec((rows, width), lambda i: (i, 0))
    const_spec = lambda shape: pl.BlockSpec(shape, lambda i: (0, 0))
    return pl.pallas_call(
        _attn_out_kernel,
        grid=(t // rows,),
        in_specs=[row_spec(d)] + [row_spec(d)] * N_GROUPS + [row_spec(LANES)] * N_GROUPS
                 + [const_spec(expand.shape), const_spec(w.shape)],
        out_specs=row_spec(d),
        out_shape=jax.ShapeDtypeStruct((t, d), F32),
        scratch_shapes=[pltpu.VMEM((N_GROUPS - 1, N_SLABS, rows, LANES), F32),
                        pltpu.VMEM((N_GROUPS - 1, rows, LANES), F32)],
        compiler_params=_params("parallel"),
        name="attn_out",
    )(x, *outs, *lses, expand, w)


def _mlstm_proj_kernel(x_ref, g_ref, w_ref, wg_ref, gb_ref, o_ref, gates_ref, h_ref):
    j = pl.program_id(1)

    @pl.when(j == 0)
    def _():
        h = _rms_rows(x_ref[...], g_ref[...]).astype(BF16)
        h_ref[...] = h
        gates_ref[...] = jnp.dot(h, wg_ref[...], preferred_element_type=F32) + gb_ref[...]

    o_ref[...] = jnp.dot(h_ref[...], w_ref[...], preferred_element_type=F32).astype(o_ref.dtype)


def _mlstm_proj(x, g, w, wg, gb):
    t, d = x.shape
    n = w.shape[1]
    return pl.pallas_call(
        _mlstm_proj_kernel,
        grid=(t // ROW_TILE, n // COL_TILE),
        in_specs=[
            pl.BlockSpec((ROW_TILE, d), lambda i, j: (i, 0)),
            pl.BlockSpec((1, d), lambda i, j: (0, 0)),
            pl.BlockSpec((d, COL_TILE), lambda i, j: (0, j)),
            pl.BlockSpec((d, LANES), lambda i, j: (0, 0)),
            pl.BlockSpec((1, LANES), lambda i, j: (0, 0)),
        ],
        out_specs=[pl.BlockSpec((ROW_TILE, COL_TILE), lambda i, j: (i, j)),
                   pl.BlockSpec((ROW_TILE, LANES), lambda i, j: (i, 0))],
        out_shape=[jax.ShapeDtypeStruct((t, n), BF16), jax.ShapeDtypeStruct((t, LANES), F32)],
        scratch_shapes=[pltpu.VMEM((ROW_TILE, d), BF16)],
        compiler_params=_params("parallel", "arbitrary"),
        name="mlstm_proj",
    )(x, g, w, wg, gb)


def _mlstm_kernel(qk_ref, v_ref, op_ref, gates_ref, cw_ref, cb_ref, og_ref, y_ref,
                  c_scr, n_scr, m_scr, tail_scr):
    chunk = pl.program_id(1)
    L = MLSTM_CHUNK

    @pl.when(chunk == 0)
    def _():
        c_scr[...] = jnp.zeros_like(c_scr)
        n_scr[...] = jnp.zeros_like(n_scr)
        m_scr[...] = jnp.zeros_like(m_scr)
        tail_scr[...] = jnp.zeros_like(tail_scr)

    cw = cw_ref[...]
    lane_id = lax.broadcasted_iota(jnp.int32, (L, LANES), 1)
    row_id = lax.broadcasted_iota(jnp.int32, (LANES, L), 0)
    t_id = lax.broadcasted_iota(jnp.int32, (L, L), 0)
    s_id = lax.broadcasted_iota(jnp.int32, (L, L), 1)
    causal = s_id <= t_id
    qk_width = MLSTM_HEADS * MLSTM_QK_DIM

    for bi in range(qk_ref.shape[0]):
        xqk = qk_ref[bi].astype(F32)
        ext = jnp.concatenate([tail_scr[bi], xqk], axis=0)
        conv = xqk * cw[CONV_WIDTH - 1:CONV_WIDTH] + cb_ref[...]
        for back in range(1, CONV_WIDTH):
            shifted = pltpu.roll(ext, back, axis=0)[SUBLANES:]
            conv = conv + shifted * cw[CONV_WIDTH - 1 - back:CONV_WIDTH - back]
        tail_scr[bi] = xqk[L - SUBLANES:]
        act = conv * _sigmoid(conv)

        gates = gates_ref[bi]
        log_f = _log_sigmoid(gates)
        gates_t = gates.T
        log_f_t = log_f.T

        for h in range(MLSTM_HEADS):
            slot = bi * MLSTM_HEADS + h
            q = act[:, h * MLSTM_QK_DIM:(h + 1) * MLSTM_QK_DIM] * (MLSTM_QK_DIM ** -0.5)
            k = act[:, qk_width + h * MLSTM_QK_DIM:qk_width + (h + 1) * MLSTM_QK_DIM]
            vcols = slice(h * MLSTM_V_DIM, (h + 1) * MLSTM_V_DIM)
            v = v_ref[bi, :, vcols]
            i_col = jnp.sum(jnp.where(lane_id == h, gates, 0.0), axis=1, keepdims=True)
            f_col = jnp.sum(jnp.where(lane_id == MLSTM_HEADS + h, log_f, 0.0), axis=1, keepdims=True)
            i_row = jnp.sum(jnp.where(row_id == h, gates_t, 0.0), axis=0, keepdims=True)
            f_row = jnp.sum(jnp.where(row_id == MLSTM_HEADS + h, log_f_t, 0.0), axis=0, keepdims=True)
            bcum_col = jnp.sum(jnp.where(causal, f_row, 0.0), axis=1, keepdims=True)
            bcum_row = jnp.sum(jnp.where(t_id <= s_id, f_col, 0.0), axis=0, keepdims=True)
            b_last = jnp.sum(f_col, axis=0, keepdims=True)
            m_prev = m_scr[slot][0:1, 0:1]
            n_prev = n_scr[slot][0:1, :]
            c_prev = c_scr[slot]

            dmat = jnp.where(causal, bcum_col - bcum_row + i_row, -jnp.inf)
            m_t = jnp.maximum(bcum_col + m_prev, jnp.max(dmat, axis=1, keepdims=True))
            inter = jnp.exp(bcum_col + m_prev - m_t)
            qb = q.astype(BF16)
            kb = k.astype(BF16)
            scores = lax.dot_general(qb, kb, (((1,), (1,)), ((), ())), preferred_element_type=F32)
            wts = scores * jnp.exp(dmat - m_t)
            num = (inter * jnp.dot(qb, c_prev.astype(BF16), preferred_element_type=F32)
                   + jnp.dot(wts.astype(BF16), v, preferred_element_type=F32))
            den = (inter * jnp.sum(q * n_prev, axis=1, keepdims=True)
                   + jnp.sum(wts, axis=1, keepdims=True))
            inv_den = 1.0 / jnp.maximum(jnp.abs(den), jnp.exp(-m_t))
            ms = (inv_den * inv_den) * jnp.mean(num * num, axis=1, keepdims=True)
            row_scale = inv_den * lax.rsqrt(ms + EPS)

            a_col = b_last - bcum_col + i_col
            m_new = jnp.maximum(b_last + m_prev, jnp.max(a_col, axis=0, keepdims=True))
            decay = jnp.exp(b_last + m_prev - m_new)
            kw = jnp.exp(a_col - m_new) * k
            c_scr[slot] = decay * c_prev + jnp.dot(kw.T.astype(BF16), v, preferred_element_type=F32)
            n_new = decay * n_prev + jnp.sum(kw, axis=0, keepdims=True)
            n_scr[slot] = jnp.broadcast_to(n_new, (SUBLANES, MLSTM_QK_DIM))
            m_scr[slot] = jnp.broadcast_to(m_new, (SUBLANES, LANES))

            gated = num * row_scale * og_ref[:, vcols] * _sigmoid(op_ref[bi, :, vcols].astype(F32))
            y_ref[bi, :, vcols] = gated.astype(y_ref.dtype)


def _mlstm_cell(proj, gates, conv_w, conv_b, out_gain, batch, seq):
    L = MLSTM_CHUNK
    nb = MLSTM_BATCH_PER_STEP
    d = D_MODEL
    slots = nb * MLSTM_HEADS
    proj3 = proj.reshape(batch, seq, proj.shape[1])
    col_spec = lambda which: pl.BlockSpec((nb, L, d), lambda b, c: (b, c, which))
    const_spec = lambda shape: pl.BlockSpec(shape, lambda b, c: (0, 0))
    y = pl.pallas_call(
        _mlstm_kernel,
        grid=(batch // nb, seq // L),
        in_specs=[col_spec(0), col_spec(1), col_spec(2),
                  pl.BlockSpec((nb, L, LANES), lambda b, c: (b, c, 0)),
                  const_spec(conv_w.shape), const_spec(conv_b.shape), const_spec(out_gain.shape)],
        out_specs=pl.BlockSpec((nb, L, d), lambda b, c: (b, c, 0)),
        out_shape=jax.ShapeDtypeStruct((batch, seq, d), BF16),
        scratch_shapes=[pltpu.VMEM((slots, MLSTM_QK_DIM, MLSTM_V_DIM), F32),
                        pltpu.VMEM((slots, SUBLANES, MLSTM_QK_DIM), F32),
                        pltpu.VMEM((slots, SUBLANES, LANES), F32),
                        pltpu.VMEM((nb, SUBLANES, d), F32)],
        compiler_params=_params("arbitrary", "arbitrary"),
        name="mlstm_cell",
    )(proj3, proj3, proj3, gates.reshape(batch, seq, LANES), conv_w, conv_b, out_gain)
    return y.reshape(batch * seq, d)


def _residual_matmul_kernel(x_ref, y_ref, w_ref, out_ref):
    out_ref[...] = x_ref[...] + jnp.dot(y_ref[...], w_ref[...], preferred_element_type=F32)


def _residual_matmul(x, y, w):
    t, d = x.shape
    row_spec = pl.BlockSpec((ROW_TILE, d), lambda i: (i, 0))
    return pl.pallas_call(
        _residual_matmul_kernel,
        grid=(t // ROW_TILE,),
        in_specs=[row_spec, row_spec, pl.BlockSpec(w.shape, lambda i: (0, 0))],
        out_specs=row_spec,
        out_shape=jax.ShapeDtypeStruct((t, d), F32),
        compiler_params=_params("parallel"),
        name="residual_matmul",
    )(x, y, w)


def _mlp_kernel(x_ref, g_ref, w1_ref, w2_ref, out_ref, h_ref):
    j = pl.program_id(1)

    @pl.when(j == 0)
    def _():
        x = x_ref[...]
        h_ref[...] = _rms_rows(x, g_ref[...]).astype(BF16)
        out_ref[...] = x

    hid = jnp.maximum(jnp.dot(h_ref[...], w1_ref[...], preferred_element_type=F32), 0.0)
    out_ref[...] += jnp.dot((hid * hid).astype(BF16), w2_ref[...], preferred_element_type=F32)


def _mlp(x, g, w1, w2):
    t, d = x.shape
    f = w1.shape[1]
    return pl.pallas_call(
        _mlp_kernel,
        grid=(t // ROW_TILE, f // COL_TILE),
        in_specs=[
            pl.BlockSpec((ROW_TILE, d), lambda i, j: (i, 0)),
            pl.BlockSpec((1, d), lambda i, j: (0, 0)),
            pl.BlockSpec((d, COL_TILE), lambda i, j: (0, j)),
            pl.BlockSpec((COL_TILE, d), lambda i, j: (j, 0)),
        ],
        out_specs=pl.BlockSpec((ROW_TILE, d), lambda i, j: (i, 0)),
        out_shape=jax.ShapeDtypeStruct((t, d), F32),
        scratch_shapes=[pltpu.VMEM((ROW_TILE, d), BF16)],
        compiler_params=_params("parallel", "arbitrary"),
        name="mlp",
    )(x, g, w1, w2)


def _attention_layer(x, batch, seq, norm_g, w_in, q_gain, k_gain, w_out, rel_bias):
    gw = ATTN_HEADS * ATTN_HEAD_DIM
    gains = jnp.concatenate([
        jnp.tile(q_gain.astype(F32) * (ATTN_HEAD_DIM ** -0.5 * LOG2_E), (1, ATTN_HEADS)).reshape(-1),
        jnp.tile(k_gain.astype(F32), (1, ATTN_HEADS)).reshape(-1),
        jnp.ones((N_GROUPS * gw,), F32)])[None, :]
    eps_row = jnp.concatenate([jnp.full((2 * N_GROUPS * gw,), EPS, F32),
                               jnp.ones((N_GROUPS * gw,), F32)])[None, :]
    head_of_lane = jnp.arange(MXU_DIM) // ATTN_HEAD_DIM
    seg = (head_of_lane[:, None] == head_of_lane[None, :]).astype(F32) * (1.0 / ATTN_HEAD_DIM)
    seg = jnp.stack([seg, jnp.zeros_like(seg)]).astype(BF16)
    qkv = _attn_proj(x, norm_g[None, :], w_in.astype(BF16), gains, eps_row, seg)

    outs, lses = [], []
    for g, (_, dilation) in enumerate(DILATION_PATTERNS):
        o, l = _attention_group(qkv, _bias_table(rel_bias[:, g], dilation), g, batch, seq)
        outs.append(o)
        lses.append(l)

    head_rows = jnp.arange(LANES)[:, None]
    head_cols = (jnp.arange(gw) // ATTN_HEAD_DIM)[None, :]
    spread = (head_rows == head_cols).astype(BF16)
    expand = jnp.concatenate([spread, spread], axis=0)
    return _attn_out(x, outs, lses, expand, w_out.astype(BF16))


def _mlstm_layer(x, batch, seq, norm_g, w_in, gate_bias, conv_w, conv_b, out_gain, w_out):
    main = 2 * MLSTM_HEADS * MLSTM_QK_DIM + MLSTM_HEADS * MLSTM_V_DIM + D_MODEL
    n_gates = 2 * MLSTM_HEADS
    wg = jnp.pad(w_in[:, main:], ((0, 0), (0, LANES - n_gates))).astype(BF16)
    gb = jnp.pad(gate_bias.astype(F32), (0, LANES - n_gates))[None, :]
    proj, gates = _mlstm_proj(x, norm_g[None, :], w_in[:, :main].astype(BF16), wg, gb)
    y = _mlstm_cell(proj, gates, conv_w.astype(F32), conv_b.astype(F32)[None, :],
                    out_gain.astype(F32).reshape(1, -1), batch, seq)
    return _residual_matmul(x, y, w_out.astype(BF16))


def kernel(x, mixer_norm, mlp_norm, rel_bias, attn_w_in, attn_q_gain, attn_k_gain, attn_w_out,
           mlstm_w_in, mlstm_gate_bias, mlstm_conv_w, mlstm_conv_b, mlstm_out_gain, mlstm_w_out,
           mlp_w_in, mlp_w_out):
    batch, seq, d = x.shape
    depth = mixer_norm.shape[0]
    h = x.reshape(batch * seq, d)
    for layer in range(depth):
        j = layer // 2
        if layer % 2 == 0:
            h = _attention_layer(h, batch, seq, mixer_norm[layer], attn_w_in[j], attn_q_gain[j],
                                 attn_k_gain[j], attn_w_out[j], rel_bias)
        else:
            h = _mlstm_layer(h, batch, seq, mixer_norm[layer], mlstm_w_in[j], mlstm_gate_bias[j],
                             mlstm_conv_w[j], mlstm_conv_b[j], mlstm_out_gain[j], mlstm_w_out[j])
        h = _mlp(h, mlp_norm[layer][None, :], mlp_w_in[layer].astype(BF16), mlp_w_out[layer].astype(BF16))
    return h.reshape(batch, seq, d)
```

```python
import math
from functools import partial

import jax
import jax.numpy as jnp
from jax import lax
from jax.experimental import pallas as pl
from jax.experimental.pallas import tpu as pltpu

F32 = jnp.float32
BF16 = jnp.bfloat16

D_MODEL = 1024
EPS = 1e-6
LOG2_E = math.log2(math.e)
LN_2 = math.log(2.0)

ATTN_HEADS = 16
ATTN_HEAD_DIM = 64
DILATION_PATTERNS = ((128, 1), (512, 4), (2048, 16))
N_GROUPS = len(DILATION_PATTERNS)
BAND = 128
REL_BUCKETS = 32
REL_MAX_DISTANCE = 2048
MASKED = -1e30
PERM_TILE = 512
ATTN_BLOCKS_PER_STEP = 4

MLSTM_HEADS = 4
MLSTM_V_DIM = 256
MLSTM_QK_DIM = 128
CONV_WIDTH = 4
MLSTM_CHUNK = 256
MLSTM_BATCH_PER_STEP = 1

LANES = 128
SUBLANES = 8
MXU_DIM = 256
VMEM_LIMIT_BYTES = 56 * 1024 * 1024

ROW_TILE = 1024
COL_TILE = 1024
N_SLABS = D_MODEL // LANES


def _params(*semantics):
    return pltpu.CompilerParams(dimension_semantics=semantics, vmem_limit_bytes=VMEM_LIMIT_BYTES)


def _rms_rows(x, gain):
    ms = jnp.mean(x * x, axis=-1, keepdims=True)
    return x * lax.rsqrt(ms + EPS) * gain


def _sigmoid(x):
    return 0.5 * jnp.tanh(0.5 * x) + 0.5


def _log_sigmoid(x):
    return jnp.minimum(x, 0.0) - jnp.log1p(jnp.exp(-jnp.abs(x)))


def _attn_proj_kernel(x_ref, g_ref, w_ref, gain_ref, eps_ref, seg_ref, o_ref,
                      xn_ref, s1_ref, h_ref, acc_a_ref, acc_b_ref, *, n_col_tiles, n_steps):
    t = pl.program_id(0)
    j = t % n_col_tiles
    d1 = DILATION_PATTERNS[1][1]
    d2 = DILATION_PATTERNS[2][1]
    assert DILATION_PATTERNS[0][1] == 1 and d2 == d1 * d1

    @pl.when(jnp.logical_and(j == 0, t < n_steps - 1))
    def _():
        xn = _rms_rows(x_ref[...], g_ref[...])
        h_ref[0] = xn.astype(BF16)
        for c in range(N_SLABS):
            xn_ref[c] = xn[:, c * LANES:(c + 1) * LANES]
        rows1 = PERM_TILE // d1
        rows2 = PERM_TILE // d2
        for base in range(0, ROW_TILE, PERM_TILE):
            for r in range(d1):
                dst = slice(base + r * rows1, base + (r + 1) * rows1)
                for c in range(N_SLABS):
                    blk = xn_ref[c, pl.ds(base + r, rows1, stride=d1), :]
                    s1_ref[c, dst, :] = blk
                    h_ref[1, dst, c * LANES:(c + 1) * LANES] = blk.astype(BF16)
            for r1 in range(d1):
                for r2 in range(d1):
                    r = r1 + d1 * r2
                    dst = slice(base + r * rows2, base + (r + 1) * rows2)
                    for c in range(N_SLABS):
                        blk = s1_ref[c, pl.ds(base + r1 * rows1 + r2, rows2, stride=d1), :]
                        h_ref[2, dst, c * LANES:(c + 1) * LANES] = blk.astype(BF16)

    @pl.when(t == 0)
    def _():
        acc_b_ref[...] = jnp.zeros_like(acc_b_ref)

    def step(acc_w_ref, acc_r_ref):
        acc_w_ref[...] = jnp.dot(h_ref[j % N_GROUPS], w_ref[...], preferred_element_type=F32)
        prev = acc_r_ref[...]
        sq = (prev * prev).astype(BF16)
        ms = jnp.concatenate(
            [jnp.dot(sq[:, c:c + MXU_DIM], seg_ref[0], preferred_element_type=F32)
             for c in range(0, COL_TILE, MXU_DIM)], axis=1)
        o_ref[...] = (prev * lax.rsqrt(ms + eps_ref[...]) * gain_ref[...]).astype(BF16)

    @pl.when(t % 2 == 0)
    def _():
        step(acc_a_ref, acc_b_ref)

    @pl.when(t % 2 == 1)
    def _():
        step(acc_b_ref, acc_a_ref)


def _attn_proj(x, g, w, gains, eps_row, seg):
    t, d = x.shape
    n = w.shape[1]
    n_col_tiles = n // COL_TILE
    n_row_tiles = t // ROW_TILE
    n_steps = n_row_tiles * n_col_tiles + 1
    n_norm_tiles = 2 * N_GROUPS * ATTN_HEADS * ATTN_HEAD_DIM // COL_TILE

    prev = lambda s: jnp.maximum(s - 1, 0)
    prev_col = lambda s: prev(s) % n_col_tiles
    return pl.pallas_call(
        partial(_attn_proj_kernel, n_col_tiles=n_col_tiles, n_steps=n_steps),
        grid=(n_steps,),
        in_specs=[
            pl.BlockSpec((ROW_TILE, d), lambda s: (jnp.minimum(s // n_col_tiles, n_row_tiles - 1), 0)),
            pl.BlockSpec((1, d), lambda s: (0, 0)),
            pl.BlockSpec((d, COL_TILE), lambda s: (0, s % n_col_tiles)),
            pl.BlockSpec((1, COL_TILE), lambda s: (0, prev_col(s))),
            pl.BlockSpec((1, COL_TILE), lambda s: (0, prev_col(s))),
            pl.BlockSpec((1, MXU_DIM, MXU_DIM), lambda s: (jnp.where(prev_col(s) < n_norm_tiles, 0, 1), 0, 0)),
        ],
        out_specs=pl.BlockSpec((ROW_TILE, COL_TILE), lambda s: (prev(s) // n_col_tiles, prev_col(s))),
        out_shape=jax.ShapeDtypeStruct((t, n), BF16),
        scratch_shapes=[pltpu.VMEM((N_SLABS, ROW_TILE, LANES), F32),
                        pltpu.VMEM((N_SLABS, ROW_TILE, LANES), F32),
                        pltpu.VMEM((N_GROUPS, ROW_TILE, d), BF16),
                        pltpu.VMEM((ROW_TILE, COL_TILE), F32),
                        pltpu.VMEM((ROW_TILE, COL_TILE), F32)],
        compiler_params=_params("arbitrary"),
        name="attn_proj",
    )(x, g, w, gains, eps_row, seg)


def _attn_kernel(q_ref, k_ref, v_ref, bias_ref, o_ref, lse_ref, den_ref, kprev_ref, vprev_ref):
    n = pl.program_id(2)
    gw = ATTN_HEADS * ATTN_HEAD_DIM

    @pl.when(n == 0)
    def _():
        kprev_ref[...] = jnp.zeros_like(kprev_ref)
        vprev_ref[...] = jnp.zeros_like(vprev_ref)

    first = (n == 0).astype(jnp.int32)
    heads_per_pass = MXU_DIM // ATTN_HEAD_DIM
    lane_head = lax.shift_right_logical(
        lax.broadcasted_iota(jnp.int32, (BAND, MXU_DIM), 1), int(math.log2(ATTN_HEAD_DIM)))
    low_half = lax.broadcasted_iota(jnp.int32, (BAND, LANES), 1) < ATTN_HEAD_DIM
    lse_lane = lax.broadcasted_iota(jnp.int32, (BAND, LANES), 1)
    n_blocks = q_ref.shape[0] * q_ref.shape[1] // BAND
    step_rows = n_blocks * BAND
    max_tiles = [jnp.zeros((BAND, LANES), F32) for _ in range(n_blocks)]
    den_tiles = [jnp.ones((BAND, LANES), F32) for _ in range(n_blocks)]

    for c in range(ATTN_HEADS // heads_per_pass):
        cols = slice(c * MXU_DIM, (c + 1) * MXU_DIM)
        q_all = q_ref[:, :, cols].reshape(step_rows, MXU_DIM)
        k_all = k_ref[:, :, cols].reshape(step_rows, MXU_DIM)
        v_all = v_ref[:, :, cols].reshape(step_rows, MXU_DIM)
        k_prev = kprev_ref[:, cols]
        v_prev = vprev_ref[:, cols]
        kprev_ref[:, cols] = k_all[step_rows - BAND:]
        vprev_ref[:, cols] = v_all[step_rows - BAND:]
        outs = []
        for blk in range(n_blocks):
            rows = slice(blk * BAND, (blk + 1) * BAND)
            qq = q_all[rows]
            kk = jnp.concatenate([k_prev, k_all[rows]], axis=0)
            vv = jnp.concatenate([v_prev, v_all[rows]], axis=0)
            k_prev, v_prev = k_all[rows], v_all[rows]
            table = first if blk == 0 else 0
            lhs = jnp.concatenate(
                [jnp.where(lane_head == hh, qq, jnp.zeros_like(qq)) for hh in range(heads_per_pass)], axis=0)
            s_all = lax.dot_general(lhs, kk, (((1,), (1,)), ((), ())), preferred_element_type=F32)
            probs = []
            for hh in range(heads_per_pass):
                h = c * heads_per_pass + hh
                s = s_all[hh * BAND:(hh + 1) * BAND] + bias_ref[table, h]
                m = jnp.max(s, axis=-1, keepdims=True)
                p = jnp.exp2(s - m)
                den = jnp.sum(p, axis=-1, keepdims=True)
                probs.append(p.astype(BF16))
                max_tiles[blk] = jnp.where(lse_lane == h, m, max_tiles[blk])
                den_tiles[blk] = jnp.where(lse_lane == h, den, den_tiles[blk])
            o_all = jnp.dot(jnp.concatenate(probs, axis=0), vv, preferred_element_type=F32)
            halves = []
            for half in range(MXU_DIM // LANES):
                lanes = slice(half * LANES, (half + 1) * LANES)
                h0 = 2 * half
                halves.append(jnp.where(low_half, o_all[h0 * BAND:(h0 + 1) * BAND, lanes],
                                        o_all[(h0 + 1) * BAND:(h0 + 2) * BAND, lanes]))
            outs.append(jnp.concatenate(halves, axis=1).astype(BF16))
        o_ref[:, :, cols] = jnp.concatenate(outs, axis=0).reshape(o_ref.shape[0], o_ref.shape[1], MXU_DIM)

    den = jnp.concatenate(den_tiles, axis=0)
    lse = (jnp.concatenate(max_tiles, axis=0) + jnp.log2(den)) * LN_2
    lse_ref[...] = lse.reshape(lse_ref.shape)
    den_ref[...] = den.reshape(den_ref.shape)


def _attention_group(qkv, bias, group, batch, seq):
    dilation = DILATION_PATTERNS[group][1]
    gw = ATTN_HEADS * ATTN_HEAD_DIM
    tiles_per_seq = seq // PERM_TILE
    rows_per_residue = PERM_TILE // dilation
    n_tiles = batch * tiles_per_seq

    step_rows = ATTN_BLOCKS_PER_STEP * BAND
    if rows_per_residue >= step_rows:
        per = rows_per_residue // step_rows
        lead, rows = 1, step_rows
        row_idx = lambda b, r, n: (b * tiles_per_seq + n // per, r * per + n % per)
    else:
        lead, rows = step_rows // rows_per_residue, rows_per_residue
        row_idx = lambda b, r, n: (b * (tiles_per_seq // lead) + n, r)

    def spec(width, col):
        return pl.BlockSpec((lead, rows, width), lambda b, r, n: (*row_idx(b, r, n), col))

    view = qkv.reshape(n_tiles, PERM_TILE, qkv.shape[1])
    out, lse, den = pl.pallas_call(
        _attn_kernel,
        grid=(batch, dilation, seq // dilation // step_rows),
        in_specs=[spec(gw, group), spec(gw, N_GROUPS + group), spec(gw, 2 * N_GROUPS + group),
                  pl.BlockSpec(bias.shape, lambda b, r, n: (0, 0, 0, 0))],
        out_specs=[spec(gw, 0), spec(LANES, 0), spec(LANES, 0)],
        out_shape=[jax.ShapeDtypeStruct((n_tiles, PERM_TILE, gw), BF16),
                   jax.ShapeDtypeStruct((n_tiles, PERM_TILE, LANES), F32),
                   jax.ShapeDtypeStruct((n_tiles, PERM_TILE, LANES), F32)],
        scratch_shapes=[pltpu.VMEM((BAND, gw), BF16), pltpu.VMEM((BAND, gw), BF16)],
        compiler_params=_params("arbitrary", "arbitrary", "arbitrary"),
        name=f"attn_group{group}",
    )(view, view, view, bias)
    return (out.reshape(batch * seq, gw), lse.reshape(batch * seq, LANES),
            den.reshape(batch * seq, LANES))


def _t5_bucket(dist):
    max_exact = REL_BUCKETS // 2
    d = jnp.maximum(dist.astype(F32), 1.0)
    large = max_exact + (jnp.log(d / max_exact) / math.log(REL_MAX_DISTANCE / max_exact)
                         * (REL_BUCKETS - max_exact)).astype(jnp.int32)
    large = jnp.minimum(large, REL_BUCKETS - 1)
    return jnp.where(dist < max_exact, dist, large)


def _bias_table(rel_bias_g, dilation):
    qi = jnp.arange(BAND)[:, None]
    ki = jnp.arange(2 * BAND)[None, :]
    steps = qi - ki + BAND
    valid = (steps >= 0) & (steps <= BAND)
    bucket = _t5_bucket(jnp.clip(steps, 0, BAND) * dilation)
    onehot = bucket[None] == jnp.arange(REL_BUCKETS)[:, None, None]
    bias = jnp.sum(jnp.where(onehot[:, None], rel_bias_g.astype(F32)[:, :, None, None], 0.0), axis=0)
    bias = bias * LOG2_E
    regular = jnp.where(valid[None], bias, MASKED)
    first = jnp.where((valid & (ki >= BAND))[None], bias, MASKED)
    return jnp.stack([regular, first])


def _attn_out_kernel(x_ref, o0_ref, o1_ref, o2_ref, l0_ref, l1_ref, l2_ref, d0_ref, d1_ref, d2_ref,
                     expand_ref, w_ref, out_ref, o_scr, l_scr, d_scr):
    lses, dens, outs = [], [], []
    group_refs = ((o0_ref, l0_ref, d0_ref), (o1_ref, l1_ref, d1_ref), (o2_ref, l2_ref, d2_ref))
    for g, (o_ref, l_ref, d_ref) in enumerate(group_refs):
        dil = DILATION_PATTERNS[g][1]
        if dil == 1:
            outs.append(o_ref[...].astype(F32))
            lses.append(l_ref[...])
            dens.append(d_ref[...])
            continue
        rows = PERM_TILE // dil
        for r in range(dil):
            src = slice(r * rows, (r + 1) * rows)
            o_res = o_ref[src, :].astype(F32)
            for c in range(N_SLABS):
                o_scr[g - 1, c, pl.ds(r, rows, stride=dil), :] = o_res[:, c * LANES:(c + 1) * LANES]
            l_scr[g - 1, pl.ds(r, rows, stride=dil), :] = l_ref[src, :]
            d_scr[g - 1, pl.ds(r, rows, stride=dil), :] = d_ref[src, :]
        outs.append(jnp.concatenate([o_scr[g - 1, c] for c in range(N_SLABS)], axis=1))
        lses.append(l_scr[g - 1])
        dens.append(d_scr[g - 1])
    m = jnp.maximum(jnp.maximum(lses[0], lses[1]), lses[2])
    es = [jnp.exp(l - m) for l in lses]
    inv = 1.0 / (es[0] + es[1] + es[2])
    y = None
    for g, e in enumerate(es):
        wt = e * inv / dens[g]
        hi = wt.astype(BF16)
        lo = (wt - hi.astype(F32)).astype(BF16)
        wide = jnp.dot(jnp.concatenate([hi, lo], axis=1), expand_ref[...], preferred_element_type=F32)
        term = wide * outs[g]
        y = term if y is None else y + term
    out_ref[...] = x_ref[...] + jnp.dot(y.astype(BF16), w_ref[...], preferred_element_type=F32)


def _attn_out(x, outs, lses, dens, expand, w):
    t, d = x.shape
    rows = PERM_TILE
    row_spec = lambda width: pl.BlockSpec((rows, width), lambda i: (i, 0))
    const_spec = lambda shape: pl.BlockSpec(shape, lambda i: (0, 0))
    return pl.pallas_call(
        _attn_out_kernel,
        grid=(t // rows,),
        in_specs=[row_spec(d)] + [row_spec(d)] * N_GROUPS + [row_spec(LANES)] * (2 * N_GROUPS)
                 + [const_spec(expand.shape), const_spec(w.shape)],
        out_specs=row_spec(d),
        out_shape=jax.ShapeDtypeStruct((t, d), F32),
        scratch_shapes=[pltpu.VMEM((N_GROUPS - 1, N_SLABS, rows, LANES), F32),
                        pltpu.VMEM((N_GROUPS - 1, rows, LANES), F32),
                        pltpu.VMEM((N_GROUPS - 1, rows, LANES), F32)],
        compiler_params=_params("parallel"),
        name="attn_out",
    )(x, *outs, *lses, *dens, expand, w)


def _mlstm_proj_kernel(x_ref, g_ref, w_ref, wg_ref, gb_ref, o_ref, gates_ref, h_ref):
    j = pl.program_id(1)

    @pl.when(j == 0)
    def _():
        h = _rms_rows(x_ref[...], g_ref[...]).astype(BF16)
        h_ref[...] = h
        gates_ref[...] = jnp.dot(h, wg_ref[...], preferred_element_type=F32) + gb_ref[...]

    o_ref[...] = jnp.dot(h_ref[...], w_ref[...], preferred_element_type=F32).astype(o_ref.dtype)


def _mlstm_proj(x, g, w, wg, gb):
    t, d = x.shape
    n = w.shape[1]
    return pl.pallas_call(
        _mlstm_proj_kernel,
        grid=(t // ROW_TILE, n // COL_TILE),
        in_specs=[
            pl.BlockSpec((ROW_TILE, d), lambda i, j: (i, 0)),
            pl.BlockSpec((1, d), lambda i, j: (0, 0)),
            pl.BlockSpec((d, COL_TILE), lambda i, j: (0, j)),
            pl.BlockSpec((d, LANES), lambda i, j: (0, 0)),
            pl.BlockSpec((1, LANES), lambda i, j: (0, 0)),
        ],
        out_specs=[pl.BlockSpec((ROW_TILE, COL_TILE), lambda i, j: (i, j)),
                   pl.BlockSpec((ROW_TILE, LANES), lambda i, j: (i, 0))],
        out_shape=[jax.ShapeDtypeStruct((t, n), BF16), jax.ShapeDtypeStruct((t, LANES), F32)],
        scratch_shapes=[pltpu.VMEM((ROW_TILE, d), BF16)],
        compiler_params=_params("parallel", "arbitrary"),
        name="mlstm_proj",
    )(x, g, w, wg, gb)


def _mlstm_kernel(qk_ref, v_ref, op_ref, gates_ref, cw_ref, cb_ref, og_ref, y_ref,
                  c_scr, n_scr, m_scr, tail_scr):
    chunk = pl.program_id(1)
    L = MLSTM_CHUNK

    @pl.when(chunk == 0)
    def _():
        c_scr[...] = jnp.zeros_like(c_scr)
        n_scr[...] = jnp.zeros_like(n_scr)
        m_scr[...] = jnp.zeros_like(m_scr)
        tail_scr[...] = jnp.zeros_like(tail_scr)

    cw = cw_ref[...]
    lane_id = lax.broadcasted_iota(jnp.int32, (L, LANES), 1)
    row_id = lax.broadcasted_iota(jnp.int32, (LANES, L), 0)
    t_id = lax.broadcasted_iota(jnp.int32, (L, L), 0)
    s_id = lax.broadcasted_iota(jnp.int32, (L, L), 1)
    causal = s_id <= t_id
    qk_width = MLSTM_HEADS * MLSTM_QK_DIM

    for bi in range(qk_ref.shape[0]):
        xqk = qk_ref[bi].astype(F32)
        ext = jnp.concatenate([tail_scr[bi], xqk], axis=0)
        conv = xqk * cw[CONV_WIDTH - 1:CONV_WIDTH] + cb_ref[...]
        for back in range(1, CONV_WIDTH):
            shifted = pltpu.roll(ext, back, axis=0)[SUBLANES:]
            conv = conv + shifted * cw[CONV_WIDTH - 1 - back:CONV_WIDTH - back]
        tail_scr[bi] = xqk[L - SUBLANES:]
        act = conv * _sigmoid(conv)

        gates = gates_ref[bi]
        log_f = _log_sigmoid(gates)
        gates_t = gates.T
        log_f_t = log_f.T

        for h in range(MLSTM_HEADS):
            slot = bi * MLSTM_HEADS + h
            q = act[:, h * MLSTM_QK_DIM:(h + 1) * MLSTM_QK_DIM] * (MLSTM_QK_DIM ** -0.5)
            k = act[:, qk_width + h * MLSTM_QK_DIM:qk_width + (h + 1) * MLSTM_QK_DIM]
            vcols = slice(h * MLSTM_V_DIM, (h + 1) * MLSTM_V_DIM)
            v = v_ref[bi, :, vcols]
            i_col = jnp.sum(jnp.where(lane_id == h, gates, 0.0), axis=1, keepdims=True)
            f_col = jnp.sum(jnp.where(lane_id == MLSTM_HEADS + h, log_f, 0.0), axis=1, keepdims=True)
            i_row = jnp.sum(jnp.where(row_id == h, gates_t, 0.0), axis=0, keepdims=True)
            f_row = jnp.sum(jnp.where(row_id == MLSTM_HEADS + h, log_f_t, 0.0), axis=0, keepdims=True)
            bcum_col = jnp.sum(jnp.where(causal, f_row, 0.0), axis=1, keepdims=True)
            bcum_row = jnp.sum(jnp.where(t_id <= s_id, f_col, 0.0), axis=0, keepdims=True)
            b_last = jnp.sum(f_col, axis=0, keepdims=True)
            m_prev = m_scr[slot][0:1, 0:1]
            n_prev = n_scr[slot][0:1, :]
            c_prev = c_scr[slot]

            dmat = jnp.where(causal, bcum_col - bcum_row + i_row, -jnp.inf)
            m_t = jnp.maximum(bcum_col + m_prev, jnp.max(dmat, axis=1, keepdims=True))
            inter = jnp.exp(bcum_col + m_prev - m_t)
            qb = q.astype(BF16)
            kb = k.astype(BF16)
            scores = lax.dot_general(qb, kb, (((1,), (1,)), ((), ())), preferred_element_type=F32)
            wts = scores * jnp.exp(dmat - m_t)
            num = (inter * jnp.dot(qb, c_prev.astype(BF16), preferred_element_type=F32)
                   + jnp.dot(wts.astype(BF16), v, preferred_element_type=F32))
            den = (inter * jnp.sum(q * n_prev, axis=1, keepdims=True)
                   + jnp.sum(wts, axis=1, keepdims=True))
            inv_den = 1.0 / jnp.maximum(jnp.abs(den), jnp.exp(-m_t))
            ms = (inv_den * inv_den) * jnp.mean(num * num, axis=1, keepdims=True)
            row_scale = inv_den * lax.rsqrt(ms + EPS)

            a_col = b_last - bcum_col + i_col
            m_new = jnp.maximum(b_last + m_prev, jnp.max(a_col, axis=0, keepdims=True))
            decay = jnp.exp(b_last + m_prev - m_new)
            kw = jnp.exp(a_col - m_new) * k
            c_scr[slot] = decay * c_prev + jnp.dot(kw.T.astype(BF16), v, preferred_element_type=F32)
            n_new = decay * n_prev + jnp.sum(kw, axis=0, keepdims=True)
            n_scr[slot] = jnp.broadcast_to(n_new, (SUBLANES, MLSTM_QK_DIM))
            m_scr[slot] = jnp.broadcast_to(m_new, (SUBLANES, LANES))

            gated = num * row_scale * og_ref[:, vcols] * _sigmoid(op_ref[bi, :, vcols].astype(F32))
            y_ref[bi, :, vcols] = gated.astype(y_ref.dtype)


def _mlstm_cell(proj, gates, conv_w, conv_b, out_gain, batch, seq):
    L = MLSTM_CHUNK
    nb = MLSTM_BATCH_PER_STEP
    d = D_MODEL
    slots = nb * MLSTM_HEADS
    proj3 = proj.reshape(batch, seq, proj.shape[1])
    col_spec = lambda which: pl.BlockSpec((nb, L, d), lambda b, c: (b, c, which))
    const_spec = lambda shape: pl.BlockSpec(shape, lambda b, c: (0, 0))
    y = pl.pallas_call(
        _mlstm_kernel,
        grid=(batch // nb, seq // L),
        in_specs=[col_spec(0), col_spec(1), col_spec(2),
                  pl.BlockSpec((nb, L, LANES), lambda b, c: (b, c, 0)),
                  const_spec(conv_w.shape), const_spec(conv_b.shape), const_spec(out_gain.shape)],
        out_specs=pl.BlockSpec((nb, L, d), lambda b, c: (b, c, 0)),
        out_shape=jax.ShapeDtypeStruct((batch, seq, d), BF16),
        scratch_shapes=[pltpu.VMEM((slots, MLSTM_QK_DIM, MLSTM_V_DIM), F32),
                        pltpu.VMEM((slots, SUBLANES, MLSTM_QK_DIM), F32),
                        pltpu.VMEM((slots, SUBLANES, LANES), F32),
                        pltpu.VMEM((nb, SUBLANES, d), F32)],
        compiler_params=_params("arbitrary", "arbitrary"),
        name="mlstm_cell",
    )(proj3, proj3, proj3, gates.reshape(batch, seq, LANES), conv_w, conv_b, out_gain)
    return y.reshape(batch * seq, d)


def _residual_matmul_kernel(x_ref, y_ref, w_ref, out_ref):
    out_ref[...] = x_ref[...] + jnp.dot(y_ref[...], w_ref[...], preferred_element_type=F32)


def _residual_matmul(x, y, w):
    t, d = x.shape
    row_spec = pl.BlockSpec((ROW_TILE, d), lambda i: (i, 0))
    return pl.pallas_call(
        _residual_matmul_kernel,
        grid=(t // ROW_TILE,),
        in_specs=[row_spec, row_spec, pl.BlockSpec(w.shape, lambda i: (0, 0))],
        out_specs=row_spec,
        out_shape=jax.ShapeDtypeStruct((t, d), F32),
        compiler_params=_params("parallel"),
        name="residual_matmul",
    )(x, y, w)


def _mlp_kernel(x_ref, g_ref, w1_ref, w2_ref, out_ref, h_ref):
    j = pl.program_id(1)

    @pl.when(j == 0)
    def _():
        x = x_ref[...]
        h_ref[...] = _rms_rows(x, g_ref[...]).astype(BF16)
        out_ref[...] = x

    hid = jnp.maximum(jnp.dot(h_ref[...], w1_ref[...], preferred_element_type=F32), 0.0)
    out_ref[...] += jnp.dot((hid * hid).astype(BF16), w2_ref[...], preferred_element_type=F32)


def _mlp(x, g, w1, w2):
    t, d = x.shape
    f = w1.shape[1]
    return pl.pallas_call(
        _mlp_kernel,
        grid=(t // ROW_TILE, f // COL_TILE),
        in_specs=[
            pl.BlockSpec((ROW_TILE, d), lambda i, j: (i, 0)),
            pl.BlockSpec((1, d), lambda i, j: (0, 0)),
            pl.BlockSpec((d, COL_TILE), lambda i, j: (0, j)),
            pl.BlockSpec((COL_TILE, d), lambda i, j: (j, 0)),
        ],
        out_specs=pl.BlockSpec((ROW_TILE, d), lambda i, j: (i, 0)),
        out_shape=jax.ShapeDtypeStruct((t, d), F32),
        scratch_shapes=[pltpu.VMEM((ROW_TILE, d), BF16)],
        compiler_params=_params("parallel", "arbitrary"),
        name="mlp",
    )(x, g, w1, w2)


def _attention_layer(x, batch, seq, norm_g, w_in, q_gain, k_gain, w_out, rel_bias):
    gw = ATTN_HEADS * ATTN_HEAD_DIM
    gains = jnp.concatenate([
        jnp.tile(q_gain.astype(F32) * (ATTN_HEAD_DIM ** -0.5 * LOG2_E), (1, ATTN_HEADS)).reshape(-1),
        jnp.tile(k_gain.astype(F32), (1, ATTN_HEADS)).reshape(-1),
        jnp.ones((N_GROUPS * gw,), F32)])[None, :]
    eps_row = jnp.concatenate([jnp.full((2 * N_GROUPS * gw,), EPS, F32),
                               jnp.ones((N_GROUPS * gw,), F32)])[None, :]
    head_of_lane = jnp.arange(MXU_DIM) // ATTN_HEAD_DIM
    seg = (head_of_lane[:, None] == head_of_lane[None, :]).astype(F32) * (1.0 / ATTN_HEAD_DIM)
    seg = jnp.stack([seg, jnp.zeros_like(seg)]).astype(BF16)
    qkv = _attn_proj(x, norm_g[None, :], w_in.astype(BF16), gains, eps_row, seg)

    outs, lses, dens = [], [], []
    for g, (_, dilation) in enumerate(DILATION_PATTERNS):
        o, l, dn = _attention_group(qkv, _bias_table(rel_bias[:, g], dilation), g, batch, seq)
        outs.append(o)
        lses.append(l)
        dens.append(dn)

    head_rows = jnp.arange(LANES)[:, None]
    head_cols = (jnp.arange(gw) // ATTN_HEAD_DIM)[None, :]
    spread = (head_rows == head_cols).astype(BF16)
    expand = jnp.concatenate([spread, spread], axis=0)
    return _attn_out(x, outs, lses, dens, expand, w_out.astype(BF16))


def _mlstm_layer(x, batch, seq, norm_g, w_in, gate_bias, conv_w, conv_b, out_gain, w_out):
    main = 2 * MLSTM_HEADS * MLSTM_QK_DIM + MLSTM_HEADS * MLSTM_V_DIM + D_MODEL
    n_gates = 2 * MLSTM_HEADS
    wg = jnp.pad(w_in[:, main:], ((0, 0), (0, LANES - n_gates))).astype(BF16)
    gb = jnp.pad(gate_bias.astype(F32), (0, LANES - n_gates))[None, :]
    proj, gates = _mlstm_proj(x, norm_g[None, :], w_in[:, :main].astype(BF16), wg, gb)
    y = _mlstm_cell(proj, gates, conv_w.astype(F32), conv_b.astype(F32)[None, :],
                    out_gain.astype(F32).reshape(1, -1), batch, seq)
    return _residual_matmul(x, y, w_out.astype(BF16))


def kernel(x, mixer_norm, mlp_norm, rel_bias, attn_w_in, attn_q_gain, attn_k_gain, attn_w_out,
           mlstm_w_in, mlstm_gate_bias, mlstm_conv_w, mlstm_conv_b, mlstm_out_gain, mlstm_w_out,
           mlp_w_in, mlp_w_out):
    batch, seq, d = x.shape
    depth = mixer_norm.shape[0]
    h = x.reshape(batch * seq, d)
    for layer in range(depth):
        j = layer // 2
        if layer % 2 == 0:
            h = _attention_layer(h, batch, seq, mixer_norm[layer], attn_w_in[j], attn_q_gain[j],
                                 attn_k_gain[j], attn_w_out[j], rel_bias)
        else:
            h = _mlstm_layer(h, batch, seq, mixer_norm[layer], mlstm_w_in[j], mlstm_gate_bias[j],
                             mlstm_conv_w[j], mlstm_conv_b[j], mlstm_out_gain[j], mlstm_w_out[j])
        h = _mlp(h, mlp_norm[layer][None, :], mlp_w_in[layer].astype(BF16), mlp_w_out[layer].astype(BF16))
    return h.reshape(batch, seq, d)
```

```python
import math
from functools import partial

import jax
import jax.numpy as jnp
from jax import lax
from jax.experimental import pallas as pl
from jax.experimental.pallas import tpu as pltpu

F32 = jnp.float32
BF16 = jnp.bfloat16

D_MODEL = 1024
EPS = 1e-6
LOG2_E = math.log2(math.e)
LN_2 = math.log(2.0)

ATTN_HEADS = 16
ATTN_HEAD_DIM = 64
DILATION_PATTERNS = ((128, 1), (512, 4), (2048, 16))
N_GROUPS = len(DILATION_PATTERNS)
BAND = 128
REL_BUCKETS = 32
REL_MAX_DISTANCE = 2048
MASKED = -1e30
PERM_TILE = 512
ATTN_BLOCKS_PER_STEP = 4

MLSTM_HEADS = 4
MLSTM_V_DIM = 256
MLSTM_QK_DIM = 128
CONV_WIDTH = 4
MLSTM_CHUNK = 256
MLSTM_BATCH_PER_STEP = 1

LANES = 128
SUBLANES = 8
MXU_DIM = 256
VMEM_LIMIT_BYTES = 56 * 1024 * 1024

ROW_TILE = 1024
COL_TILE = 1024
N_SLABS = D_MODEL // LANES


def _params(*semantics):
    return pltpu.CompilerParams(dimension_semantics=semantics, vmem_limit_bytes=VMEM_LIMIT_BYTES)


def _rms_rows(x, gain):
    ms = jnp.mean(x * x, axis=-1, keepdims=True)
    return x * lax.rsqrt(ms + EPS) * gain


def _sigmoid(x):
    return 0.5 * jnp.tanh(0.5 * x) + 0.5


def _log_sigmoid(x):
    return jnp.minimum(x, 0.0) - jnp.log1p(jnp.exp(-jnp.abs(x)))


def _attn_proj_kernel(x_ref, g_ref, w_ref, gain_ref, seg_ref, o_ref,
                      xn_ref, s1_ref, h_ref, acc_a_ref, acc_b_ref, *, n_col_tiles, n_norm_tiles, n_steps):
    t = pl.program_id(0)
    j = t % n_col_tiles
    d1 = DILATION_PATTERNS[1][1]
    d2 = DILATION_PATTERNS[2][1]
    assert DILATION_PATTERNS[0][1] == 1 and d2 == d1 * d1

    @pl.when(jnp.logical_and(j == 0, t < n_steps - 1))
    def _():
        xn = _rms_rows(x_ref[...], g_ref[...])
        h_ref[0] = xn.astype(BF16)
        for c in range(N_SLABS):
            xn_ref[c] = xn[:, c * LANES:(c + 1) * LANES]
        rows1 = PERM_TILE // d1
        rows2 = PERM_TILE // d2
        for base in range(0, ROW_TILE, PERM_TILE):
            for r in range(d1):
                dst = slice(base + r * rows1, base + (r + 1) * rows1)
                for c in range(N_SLABS):
                    blk = xn_ref[c, pl.ds(base + r, rows1, stride=d1), :]
                    s1_ref[c, dst, :] = blk
                    h_ref[1, dst, c * LANES:(c + 1) * LANES] = blk.astype(BF16)
            for r1 in range(d1):
                for r2 in range(d1):
                    r = r1 + d1 * r2
                    dst = slice(base + r * rows2, base + (r + 1) * rows2)
                    for c in range(N_SLABS):
                        blk = s1_ref[c, pl.ds(base + r1 * rows1 + r2, rows2, stride=d1), :]
                        h_ref[2, dst, c * LANES:(c + 1) * LANES] = blk.astype(BF16)

    @pl.when(t == 0)
    def _():
        acc_b_ref[...] = jnp.zeros_like(acc_b_ref)

    def step(acc_w_ref, acc_r_ref, normalise):
        acc_w_ref[...] = jnp.dot(h_ref[j % N_GROUPS], w_ref[...], preferred_element_type=F32)
        prev = acc_r_ref[...]
        if not normalise:
            o_ref[...] = prev.astype(BF16)
            return
        sq = (prev * prev).astype(BF16)
        ms = jnp.concatenate(
            [jnp.dot(sq[:, c:c + MXU_DIM], seg_ref[...], preferred_element_type=F32)
             for c in range(0, COL_TILE, MXU_DIM)], axis=1)
        o_ref[...] = (prev * lax.rsqrt(ms + EPS) * gain_ref[...]).astype(BF16)

    prev_is_qk = (jnp.maximum(t - 1, 0) % n_col_tiles) < n_norm_tiles
    for parity, (acc_w_ref, acc_r_ref) in enumerate(((acc_a_ref, acc_b_ref), (acc_b_ref, acc_a_ref))):
        for normalise in (True, False):
            @pl.when(jnp.logical_and(t % 2 == parity, prev_is_qk == normalise))
            def _(acc_w_ref=acc_w_ref, acc_r_ref=acc_r_ref, normalise=normalise):
                step(acc_w_ref, acc_r_ref, normalise)


def _attn_proj(x, g, w, gains, seg):
    t, d = x.shape
    n = w.shape[1]
    n_col_tiles = n // COL_TILE
    n_row_tiles = t // ROW_TILE
    n_steps = n_row_tiles * n_col_tiles + 1
    n_norm_tiles = 2 * N_GROUPS * ATTN_HEADS * ATTN_HEAD_DIM // COL_TILE

    prev = lambda s: jnp.maximum(s - 1, 0)
    prev_col = lambda s: prev(s) % n_col_tiles
    return pl.pallas_call(
        partial(_attn_proj_kernel, n_col_tiles=n_col_tiles, n_norm_tiles=n_norm_tiles, n_steps=n_steps),
        grid=(n_steps,),
        in_specs=[
            pl.BlockSpec((ROW_TILE, d), lambda s: (jnp.minimum(s // n_col_tiles, n_row_tiles - 1), 0)),
            pl.BlockSpec((1, d), lambda s: (0, 0)),
            pl.BlockSpec((d, COL_TILE), lambda s: (0, s % n_col_tiles)),
            pl.BlockSpec((1, COL_TILE), lambda s: (0, prev_col(s))),
            pl.BlockSpec((MXU_DIM, MXU_DIM), lambda s: (0, 0)),
        ],
        out_specs=pl.BlockSpec((ROW_TILE, COL_TILE), lambda s: (prev(s) // n_col_tiles, prev_col(s))),
        out_shape=jax.ShapeDtypeStruct((t, n), BF16),
        scratch_shapes=[pltpu.VMEM((N_SLABS, ROW_TILE, LANES), F32),
                        pltpu.VMEM((N_SLABS, ROW_TILE, LANES), F32),
                        pltpu.VMEM((N_GROUPS, ROW_TILE, d), BF16),
                        pltpu.VMEM((ROW_TILE, COL_TILE), F32),
                        pltpu.VMEM((ROW_TILE, COL_TILE), F32)],
        compiler_params=_params("arbitrary"),
        name="attn_proj",
    )(x, g, w, gains, seg)


def _attn_kernel(q_ref, k_ref, v_ref, bias_ref, o_ref, lse_ref, den_ref, kprev_ref, vprev_ref):
    n = pl.program_id(2)
    gw = ATTN_HEADS * ATTN_HEAD_DIM

    @pl.when(n == 0)
    def _():
        kprev_ref[...] = jnp.zeros_like(kprev_ref)
        vprev_ref[...] = jnp.zeros_like(vprev_ref)

    first = (n == 0).astype(jnp.int32)
    heads_per_pass = MXU_DIM // ATTN_HEAD_DIM
    lane_head = lax.shift_right_logical(
        lax.broadcasted_iota(jnp.int32, (BAND, MXU_DIM), 1), int(math.log2(ATTN_HEAD_DIM)))
    low_half = lax.broadcasted_iota(jnp.int32, (BAND, LANES), 1) < ATTN_HEAD_DIM
    lse_lane = lax.broadcasted_iota(jnp.int32, (BAND, LANES), 1)
    n_blocks = q_ref.shape[0] * q_ref.shape[1] // BAND
    step_rows = n_blocks * BAND
    max_tiles = [jnp.zeros((BAND, LANES), F32) for _ in range(n_blocks)]
    den_tiles = [jnp.ones((BAND, LANES), F32) for _ in range(n_blocks)]

    for c in range(ATTN_HEADS // heads_per_pass):
        cols = slice(c * MXU_DIM, (c + 1) * MXU_DIM)
        q_all = q_ref[:, :, cols].reshape(step_rows, MXU_DIM)
        k_all = k_ref[:, :, cols].reshape(step_rows, MXU_DIM)
        v_all = v_ref[:, :, cols].reshape(step_rows, MXU_DIM)
        k_prev = kprev_ref[:, cols]
        v_prev = vprev_ref[:, cols]
        kprev_ref[:, cols] = k_all[step_rows - BAND:]
        vprev_ref[:, cols] = v_all[step_rows - BAND:]
        outs = []
        for blk in range(n_blocks):
            rows = slice(blk * BAND, (blk + 1) * BAND)
            qq = q_all[rows]
            kk = jnp.concatenate([k_prev, k_all[rows]], axis=0)
            vv = jnp.concatenate([v_prev, v_all[rows]], axis=0)
            k_prev, v_prev = k_all[rows], v_all[rows]
            table = first if blk == 0 else 0
            lhs = jnp.concatenate(
                [jnp.where(lane_head == hh, qq, jnp.zeros_like(qq)) for hh in range(heads_per_pass)], axis=0)
            s_all = lax.dot_general(lhs, kk, (((1,), (1,)), ((), ())), preferred_element_type=F32)
            probs = []
            for hh in range(heads_per_pass):
                h = c * heads_per_pass + hh
                s = s_all[hh * BAND:(hh + 1) * BAND] + bias_ref[table, h]
                m = jnp.max(s, axis=-1, keepdims=True)
                p = jnp.exp2(s - m)
                den = jnp.sum(p, axis=-1, keepdims=True)
                probs.append(p.astype(BF16))
                max_tiles[blk] = jnp.where(lse_lane == h, m, max_tiles[blk])
                den_tiles[blk] = jnp.where(lse_lane == h, den, den_tiles[blk])
            o_all = jnp.dot(jnp.concatenate(probs, axis=0), vv, preferred_element_type=F32)
            halves = []
            for half in range(MXU_DIM // LANES):
                lanes = slice(half * LANES, (half + 1) * LANES)
                h0 = 2 * half
                halves.append(jnp.where(low_half, o_all[h0 * BAND:(h0 + 1) * BAND, lanes],
                                        o_all[(h0 + 1) * BAND:(h0 + 2) * BAND, lanes]))
            outs.append(jnp.concatenate(halves, axis=1).astype(BF16))
        o_ref[:, :, cols] = jnp.concatenate(outs, axis=0).reshape(o_ref.shape[0], o_ref.shape[1], MXU_DIM)

    den = jnp.concatenate(den_tiles, axis=0)
    lse = (jnp.concatenate(max_tiles, axis=0) + jnp.log2(den)) * LN_2
    lse_ref[...] = lse.reshape(lse_ref.shape)
    den_ref[...] = den.reshape(den_ref.shape)


def _attention_group(qkv, bias, group, batch, seq):
    dilation = DILATION_PATTERNS[group][1]
    gw = ATTN_HEADS * ATTN_HEAD_DIM
    tiles_per_seq = seq // PERM_TILE
    rows_per_residue = PERM_TILE // dilation
    n_tiles = batch * tiles_per_seq

    step_rows = ATTN_BLOCKS_PER_STEP * BAND
    if rows_per_residue >= step_rows:
        per = rows_per_residue // step_rows
        lead, rows = 1, step_rows
        row_idx = lambda b, r, n: (b * tiles_per_seq + n // per, r * per + n % per)
    else:
        lead, rows = step_rows // rows_per_residue, rows_per_residue
        row_idx = lambda b, r, n: (b * (tiles_per_seq // lead) + n, r)

    def spec(width, col):
        return pl.BlockSpec((lead, rows, width), lambda b, r, n: (*row_idx(b, r, n), col))

    view = qkv.reshape(n_tiles, PERM_TILE, qkv.shape[1])
    out, lse, den = pl.pallas_call(
        _attn_kernel,
        grid=(batch, dilation, seq // dilation // step_rows),
        in_specs=[spec(gw, group), spec(gw, N_GROUPS + group), spec(gw, 2 * N_GROUPS + group),
                  pl.BlockSpec(bias.shape, lambda b, r, n: (0, 0, 0, 0))],
        out_specs=[spec(gw, 0), spec(LANES, 0), spec(LANES, 0)],
        out_shape=[jax.ShapeDtypeStruct((n_tiles, PERM_TILE, gw), BF16),
                   jax.ShapeDtypeStruct((n_tiles, PERM_TILE, LANES), F32),
                   jax.ShapeDtypeStruct((n_tiles, PERM_TILE, LANES), F32)],
        scratch_shapes=[pltpu.VMEM((BAND, gw), BF16), pltpu.VMEM((BAND, gw), BF16)],
        compiler_params=_params("arbitrary", "arbitrary", "arbitrary"),
        name=f"attn_group{group}",
    )(view, view, view, bias)
    return (out.reshape(batch * seq, gw), lse.reshape(batch * seq, LANES),
            den.reshape(batch * seq, LANES))


def _t5_bucket(dist):
    max_exact = REL_BUCKETS // 2
    d = jnp.maximum(dist.astype(F32), 1.0)
    large = max_exact + (jnp.log(d / max_exact) / math.log(REL_MAX_DISTANCE / max_exact)
                         * (REL_BUCKETS - max_exact)).astype(jnp.int32)
    large = jnp.minimum(large, REL_BUCKETS - 1)
    return jnp.where(dist < max_exact, dist, large)


def _bias_table(rel_bias_g, dilation):
    qi = jnp.arange(BAND)[:, None]
    ki = jnp.arange(2 * BAND)[None, :]
    steps = qi - ki + BAND
    valid = (steps >= 0) & (steps <= BAND)
    bucket = _t5_bucket(jnp.clip(steps, 0, BAND) * dilation)
    onehot = bucket[None] == jnp.arange(REL_BUCKETS)[:, None, None]
    bias = jnp.sum(jnp.where(onehot[:, None], rel_bias_g.astype(F32)[:, :, None, None], 0.0), axis=0)
    bias = bias * LOG2_E
    regular = jnp.where(valid[None], bias, MASKED)
    first = jnp.where((valid & (ki >= BAND))[None], bias, MASKED)
    return jnp.stack([regular, first])


def _attn_out_kernel(x_ref, o0_ref, o1_ref, o2_ref, l0_ref, l1_ref, l2_ref, d0_ref, d1_ref, d2_ref,
                     expand_ref, w_ref, out_ref, o_scr, l_scr, d_scr):
    lses, dens, outs = [], [], []
    group_refs = ((o0_ref, l0_ref, d0_ref), (o1_ref, l1_ref, d1_ref), (o2_ref, l2_ref, d2_ref))
    for g, (o_ref, l_ref, d_ref) in enumerate(group_refs):
        dil = DILATION_PATTERNS[g][1]
        if dil == 1:
            outs.append(o_ref[...].astype(F32))
            lses.append(l_ref[...])
            dens.append(d_ref[...])
            continue
        rows = PERM_TILE // dil
        for r in range(dil):
            src = slice(r * rows, (r + 1) * rows)
            o_res = o_ref[src, :].astype(F32)
            for c in range(N_SLABS):
                o_scr[g - 1, c, pl.ds(r, rows, stride=dil), :] = o_res[:, c * LANES:(c + 1) * LANES]
            l_scr[g - 1, pl.ds(r, rows, stride=dil), :] = l_ref[src, :]
            d_scr[g - 1, pl.ds(r, rows, stride=dil), :] = d_ref[src, :]
        outs.append(jnp.concatenate([o_scr[g - 1, c] for c in range(N_SLABS)], axis=1))
        lses.append(l_scr[g - 1])
        dens.append(d_scr[g - 1])
    m = jnp.maximum(jnp.maximum(lses[0], lses[1]), lses[2])
    es = [jnp.exp(l - m) for l in lses]
    inv = 1.0 / (es[0] + es[1] + es[2])
    y = None
    for g, e in enumerate(es):
        wt = e * inv / dens[g]
        hi = wt.astype(BF16)
        lo = (wt - hi.astype(F32)).astype(BF16)
        wide = jnp.dot(jnp.concatenate([hi, lo], axis=1), expand_ref[...], preferred_element_type=F32)
        term = wide * outs[g]
        y = term if y is None else y + term
    out_ref[...] = x_ref[...] + jnp.dot(y.astype(BF16), w_ref[...], preferred_element_type=F32)


def _attn_out(x, outs, lses, dens, expand, w):
    t, d = x.shape
    rows = PERM_TILE
    row_spec = lambda width: pl.BlockSpec((rows, width), lambda i: (i, 0))
    const_spec = lambda shape: pl.BlockSpec(shape, lambda i: (0, 0))
    return pl.pallas_call(
        _attn_out_kernel,
        grid=(t // rows,),
        in_specs=[row_spec(d)] + [row_spec(d)] * N_GROUPS + [row_spec(LANES)] * (2 * N_GROUPS)
                 + [const_spec(expand.shape), const_spec(w.shape)],
        out_specs=row_spec(d),
        out_shape=jax.ShapeDtypeStruct((t, d), F32),
        scratch_shapes=[pltpu.VMEM((N_GROUPS - 1, N_SLABS, rows, LANES), F32),
                        pltpu.VMEM((N_GROUPS - 1, rows, LANES), F32),
                        pltpu.VMEM((N_GROUPS - 1, rows, LANES), F32)],
        compiler_params=_params("parallel"),
        name="attn_out",
    )(x, *outs, *lses, *dens, expand, w)


def _mlstm_proj_kernel(x_ref, g_ref, w_ref, wg_ref, gb_ref, o_ref, gates_ref, h_ref):
    j = pl.program_id(1)

    @pl.when(j == 0)
    def _():
        h = _rms_rows(x_ref[...], g_ref[...]).astype(BF16)
        h_ref[...] = h
        gates_ref[...] = jnp.dot(h, wg_ref[...], preferred_element_type=F32) + gb_ref[...]

    o_ref[...] = jnp.dot(h_ref[...], w_ref[...], preferred_element_type=F32).astype(o_ref.dtype)


def _mlstm_proj(x, g, w, wg, gb):
    t, d = x.shape
    n = w.shape[1]
    return pl.pallas_call(
        _mlstm_proj_kernel,
        grid=(t // ROW_TILE, n // COL_TILE),
        in_specs=[
            pl.BlockSpec((ROW_TILE, d), lambda i, j: (i, 0)),
            pl.BlockSpec((1, d), lambda i, j: (0, 0)),
            pl.BlockSpec((d, COL_TILE), lambda i, j: (0, j)),
            pl.BlockSpec((d, LANES), lambda i, j: (0, 0)),
            pl.BlockSpec((1, LANES), lambda i, j: (0, 0)),
        ],
        out_specs=[pl.BlockSpec((ROW_TILE, COL_TILE), lambda i, j: (i, j)),
                   pl.BlockSpec((ROW_TILE, LANES), lambda i, j: (i, 0))],
        out_shape=[jax.ShapeDtypeStruct((t, n), BF16), jax.ShapeDtypeStruct((t, LANES), F32)],
        scratch_shapes=[pltpu.VMEM((ROW_TILE, d), BF16)],
        compiler_params=_params("parallel", "arbitrary"),
        name="mlstm_proj",
    )(x, g, w, wg, gb)


def _mlstm_kernel(qk_ref, v_ref, op_ref, gates_ref, cw_ref, cb_ref, og_ref, y_ref,
                  c_scr, n_scr, m_scr, tail_scr):
    chunk = pl.program_id(1)
    L = MLSTM_CHUNK

    @pl.when(chunk == 0)
    def _():
        c_scr[...] = jnp.zeros_like(c_scr)
        n_scr[...] = jnp.zeros_like(n_scr)
        m_scr[...] = jnp.zeros_like(m_scr)
        tail_scr[...] = jnp.zeros_like(tail_scr)

    cw = cw_ref[...]
    lane_id = lax.broadcasted_iota(jnp.int32, (L, LANES), 1)
    row_id = lax.broadcasted_iota(jnp.int32, (LANES, L), 0)
    t_id = lax.broadcasted_iota(jnp.int32, (L, L), 0)
    s_id = lax.broadcasted_iota(jnp.int32, (L, L), 1)
    causal = s_id <= t_id
    qk_width = MLSTM_HEADS * MLSTM_QK_DIM

    for bi in range(qk_ref.shape[0]):
        xqk = qk_ref[bi].astype(F32)
        ext = jnp.concatenate([tail_scr[bi], xqk], axis=0)
        conv = xqk * cw[CONV_WIDTH - 1:CONV_WIDTH] + cb_ref[...]
        for back in range(1, CONV_WIDTH):
            shifted = pltpu.roll(ext, back, axis=0)[SUBLANES:]
            conv = conv + shifted * cw[CONV_WIDTH - 1 - back:CONV_WIDTH - back]
        tail_scr[bi] = xqk[L - SUBLANES:]
        act = conv * _sigmoid(conv)

        gates = gates_ref[bi]
        log_f = _log_sigmoid(gates)
        gates_t = gates.T
        log_f_t = log_f.T

        for h in range(MLSTM_HEADS):
            slot = bi * MLSTM_HEADS + h
            q = act[:, h * MLSTM_QK_DIM:(h + 1) * MLSTM_QK_DIM] * (MLSTM_QK_DIM ** -0.5)
            k = act[:, qk_width + h * MLSTM_QK_DIM:qk_width + (h + 1) * MLSTM_QK_DIM]
            vcols = slice(h * MLSTM_V_DIM, (h + 1) * MLSTM_V_DIM)
            v = v_ref[bi, :, vcols]
            i_col = jnp.sum(jnp.where(lane_id == h, gates, 0.0), axis=1, keepdims=True)
            f_col = jnp.sum(jnp.where(lane_id == MLSTM_HEADS + h, log_f, 0.0), axis=1, keepdims=True)
            i_row = jnp.sum(jnp.where(row_id == h, gates_t, 0.0), axis=0, keepdims=True)
            f_row = jnp.sum(jnp.where(row_id == MLSTM_HEADS + h, log_f_t, 0.0), axis=0, keepdims=True)
            bcum_col = jnp.sum(jnp.where(causal, f_row, 0.0), axis=1, keepdims=True)
            bcum_row = jnp.sum(jnp.where(t_id <= s_id, f_col, 0.0), axis=0, keepdims=True)
            b_last = jnp.sum(f_col, axis=0, keepdims=True)
            m_prev = m_scr[slot][0:1, 0:1]
            n_prev = n_scr[slot][0:1, :]
            c_prev = c_scr[slot]

            dmat = jnp.where(causal, bcum_col - bcum_row + i_row, -jnp.inf)
            m_t = jnp.maximum(bcum_col + m_prev, jnp.max(dmat, axis=1, keepdims=True))
            inter = jnp.exp(bcum_col + m_prev - m_t)
            qb = q.astype(BF16)
            kb = k.astype(BF16)
            scores = lax.dot_general(qb, kb, (((1,), (1,)), ((), ())), preferred_element_type=F32)
            wts = scores * jnp.exp(dmat - m_t)
            num = (inter * jnp.dot(qb, c_prev.astype(BF16), preferred_element_type=F32)
                   + jnp.dot(wts.astype(BF16), v, preferred_element_type=F32))
            den = (inter * jnp.sum(q * n_prev, axis=1, keepdims=True)
                   + jnp.sum(wts, axis=1, keepdims=True))
            inv_den = 1.0 / jnp.maximum(jnp.abs(den), jnp.exp(-m_t))
            ms = (inv_den * inv_den) * jnp.mean(num * num, axis=1, keepdims=True)
            row_scale = inv_den * lax.rsqrt(ms + EPS)

            a_col = b_last - bcum_col + i_col
            m_new = jnp.maximum(b_last + m_prev, jnp.max(a_col, axis=0, keepdims=True))
            decay = jnp.exp(b_last + m_prev - m_new)
            kw = jnp.exp(a_col - m_new) * k
            c_scr[slot] = decay * c_prev + jnp.dot(kw.T.astype(BF16), v, preferred_element_type=F32)
            n_new = decay * n_prev + jnp.sum(kw, axis=0, keepdims=True)
            n_scr[slot] = jnp.broadcast_to(n_new, (SUBLANES, MLSTM_QK_DIM))
            m_scr[slot] = jnp.broadcast_to(m_new, (SUBLANES, LANES))

            gated = num * row_scale * og_ref[:, vcols] * _sigmoid(op_ref[bi, :, vcols].astype(F32))
            y_ref[bi, :, vcols] = gated.astype(y_ref.dtype)


def _mlstm_cell(proj, gates, conv_w, conv_b, out_gain, batch, seq):
    L = MLSTM_CHUNK
    nb = MLSTM_BATCH_PER_STEP
    d = D_MODEL
    slots = nb * MLSTM_HEADS
    proj3 = proj.reshape(batch, seq, proj.shape[1])
    col_spec = lambda which: pl.BlockSpec((nb, L, d), lambda b, c: (b, c, which))
    const_spec = lambda shape: pl.BlockSpec(shape, lambda b, c: (0, 0))
    y = pl.pallas_call(
        _mlstm_kernel,
        grid=(batch // nb, seq // L),
        in_specs=[col_spec(0), col_spec(1), col_spec(2),
                  pl.BlockSpec((nb, L, LANES), lambda b, c: (b, c, 0)),
                  const_spec(conv_w.shape), const_spec(conv_b.shape), const_spec(out_gain.shape)],
        out_specs=pl.BlockSpec((nb, L, d), lambda b, c: (b, c, 0)),
        out_shape=jax.ShapeDtypeStruct((batch, seq, d), BF16),
        scratch_shapes=[pltpu.VMEM((slots, MLSTM_QK_DIM, MLSTM_V_DIM), F32),
                        pltpu.VMEM((slots, SUBLANES, MLSTM_QK_DIM), F32),
                        pltpu.VMEM((slots, SUBLANES, LANES), F32),
                        pltpu.VMEM((nb, SUBLANES, d), F32)],
        compiler_params=_params("arbitrary", "arbitrary"),
        name="mlstm_cell",
    )(proj3, proj3, proj3, gates.reshape(batch, seq, LANES), conv_w, conv_b, out_gain)
    return y.reshape(batch * seq, d)


def _residual_matmul_kernel(x_ref, y_ref, w_ref, out_ref):
    out_ref[...] = x_ref[...] + jnp.dot(y_ref[...], w_ref[...], preferred_element_type=F32)


def _residual_matmul(x, y, w):
    t, d = x.shape
    row_spec = pl.BlockSpec((ROW_TILE, d), lambda i: (i, 0))
    return pl.pallas_call(
        _residual_matmul_kernel,
        grid=(t // ROW_TILE,),
        in_specs=[row_spec, row_spec, pl.BlockSpec(w.shape, lambda i: (0, 0))],
        out_specs=row_spec,
        out_shape=jax.ShapeDtypeStruct((t, d), F32),
        compiler_params=_params("parallel"),
        name="residual_matmul",
    )(x, y, w)


def _mlp_kernel(x_ref, g_ref, w1_ref, w2_ref, out_ref, h_ref):
    j = pl.program_id(1)

    @pl.when(j == 0)
    def _():
        x = x_ref[...]
        h_ref[...] = _rms_rows(x, g_ref[...]).astype(BF16)
        out_ref[...] = x

    hid = jnp.maximum(jnp.dot(h_ref[...], w1_ref[...].astype(BF16), preferred_element_type=F32), 0.0)
    out_ref[...] += jnp.dot((hid * hid).astype(BF16), w2_ref[...].astype(BF16), preferred_element_type=F32)


def _mlp(x, g, w1, w2):
    t, d = x.shape
    f = w1.shape[1]
    return pl.pallas_call(
        _mlp_kernel,
        grid=(t // ROW_TILE, f // COL_TILE),
        in_specs=[
            pl.BlockSpec((ROW_TILE, d), lambda i, j: (i, 0)),
            pl.BlockSpec((1, d), lambda i, j: (0, 0)),
            pl.BlockSpec((d, COL_TILE), lambda i, j: (0, j)),
            pl.BlockSpec((COL_TILE, d), lambda i, j: (j, 0)),
        ],
        out_specs=pl.BlockSpec((ROW_TILE, d), lambda i, j: (i, 0)),
        out_shape=jax.ShapeDtypeStruct((t, d), F32),
        scratch_shapes=[pltpu.VMEM((ROW_TILE, d), BF16)],
        compiler_params=_params("parallel", "arbitrary"),
        name="mlp",
    )(x, g, w1, w2)


def _attention_layer(x, batch, seq, norm_g, w_in, q_gain, k_gain, w_out, rel_bias):
    gw = ATTN_HEADS * ATTN_HEAD_DIM
    gains = jnp.concatenate([
        jnp.tile(q_gain.astype(F32) * (ATTN_HEAD_DIM ** -0.5 * LOG2_E), (1, ATTN_HEADS)).reshape(-1),
        jnp.tile(k_gain.astype(F32), (1, ATTN_HEADS)).reshape(-1),
        jnp.ones((N_GROUPS * gw,), F32)])[None, :]
    head_of_lane = jnp.arange(MXU_DIM) // ATTN_HEAD_DIM
    seg = ((head_of_lane[:, None] == head_of_lane[None, :]).astype(F32) * (1.0 / ATTN_HEAD_DIM)).astype(BF16)
    qkv = _attn_proj(x, norm_g[None, :], w_in.astype(BF16), gains, seg)

    outs, lses, dens = [], [], []
    for g, (_, dilation) in enumerate(DILATION_PATTERNS):
        o, l, dn = _attention_group(qkv, _bias_table(rel_bias[:, g], dilation), g, batch, seq)
        outs.append(o)
        lses.append(l)
        dens.append(dn)

    head_rows = jnp.arange(LANES)[:, None]
    head_cols = (jnp.arange(gw) // ATTN_HEAD_DIM)[None, :]
    spread = (head_rows == head_cols).astype(BF16)
    expand = jnp.concatenate([spread, spread], axis=0)
    return _attn_out(x, outs, lses, dens, expand, w_out.astype(BF16))


def _mlstm_layer(x, batch, seq, norm_g, w_in, gate_bias, conv_w, conv_b, out_gain, w_out):
    main = 2 * MLSTM_HEADS * MLSTM_QK_DIM + MLSTM_HEADS * MLSTM_V_DIM + D_MODEL
    n_gates = 2 * MLSTM_HEADS
    wg = jnp.pad(w_in[:, main:], ((0, 0), (0, LANES - n_gates))).astype(BF16)
    gb = jnp.pad(gate_bias.astype(F32), (0, LANES - n_gates))[None, :]
    proj, gates = _mlstm_proj(x, norm_g[None, :], w_in[:, :main].astype(BF16), wg, gb)
    y = _mlstm_cell(proj, gates, conv_w.astype(F32), conv_b.astype(F32)[None, :],
                    out_gain.astype(F32).reshape(1, -1), batch, seq)
    return _residual_matmul(x, y, w_out.astype(BF16))


def kernel(x, mixer_norm, mlp_norm, rel_bias, attn_w_in, attn_q_gain, attn_k_gain, attn_w_out,
           mlstm_w_in, mlstm_gate_bias, mlstm_conv_w, mlstm_conv_b, mlstm_out_gain, mlstm_w_out,
           mlp_w_in, mlp_w_out):
    batch, seq, d = x.shape
    depth = mixer_norm.shape[0]
    h = x.reshape(batch * seq, d)
    for layer in range(depth):
        j = layer // 2
        if layer % 2 == 0:
            h = _attention_layer(h, batch, seq, mixer_norm[layer], attn_w_in[j], attn_q_gain[j],
                                 attn_k_gain[j], attn_w_out[j], rel_bias)
        else:
            h = _mlstm_layer(h, batch, seq, mixer_norm[layer], mlstm_w_in[j], mlstm_gate_bias[j],
                             mlstm_conv_w[j], mlstm_conv_b[j], mlstm_out_gain[j], mlstm_w_out[j])
        h = _mlp(h, mlp_norm[layer][None, :], mlp_w_in[layer], mlp_w_out[layer])
    return h.reshape(batch, seq, d)
```

```python
import math
from functools import partial

import jax
import jax.numpy as jnp
from jax import lax
from jax.experimental import pallas as pl
from jax.experimental.pallas import tpu as pltpu

F32 = jnp.float32
BF16 = jnp.bfloat16

D_MODEL = 1024
EPS = 1e-6
LOG2_E = math.log2(math.e)
LN_2 = math.log(2.0)

ATTN_HEADS = 16
ATTN_HEAD_DIM = 64
DILATION_PATTERNS = ((128, 1), (512, 4), (2048, 16))
N_GROUPS = len(DILATION_PATTERNS)
BAND = 128
REL_BUCKETS = 32
REL_MAX_DISTANCE = 2048
MASKED = -1e30
PERM_TILE = 512
ATTN_BLOCKS_PER_STEP = 4

MLSTM_HEADS = 4
MLSTM_V_DIM = 256
MLSTM_QK_DIM = 128
CONV_WIDTH = 4
MLSTM_CHUNK = 256
MLSTM_BATCH_PER_STEP = 1

LANES = 128
SUBLANES = 8
MXU_DIM = 256
VMEM_LIMIT_BYTES = 56 * 1024 * 1024

ROW_TILE = 1024
COL_TILE = 1024
N_SLABS = D_MODEL // LANES


def _params(*semantics):
    return pltpu.CompilerParams(dimension_semantics=semantics, vmem_limit_bytes=VMEM_LIMIT_BYTES)


def _rms_rows(x, gain):
    ms = jnp.mean(x * x, axis=-1, keepdims=True)
    return x * lax.rsqrt(ms + EPS) * gain


def _sigmoid(x):
    return 0.5 * jnp.tanh(0.5 * x) + 0.5


def _log_sigmoid(x):
    return jnp.minimum(x, 0.0) - jnp.log1p(jnp.exp(-jnp.abs(x)))


def _attn_proj_kernel(x_ref, g_ref, w_ref, gain_ref, seg_ref, o_ref,
                      xn_ref, s1_ref, h_ref, acc_a_ref, acc_b_ref, *, n_col_tiles, n_norm_tiles, n_steps):
    t = pl.program_id(0)
    j = t % n_col_tiles
    d1 = DILATION_PATTERNS[1][1]
    d2 = DILATION_PATTERNS[2][1]
    assert DILATION_PATTERNS[0][1] == 1 and d2 == d1 * d1

    @pl.when(jnp.logical_and(j == 0, t < n_steps - 1))
    def _():
        xn = _rms_rows(x_ref[...], g_ref[...])
        h_ref[0] = xn.astype(BF16)
        for c in range(N_SLABS):
            xn_ref[c] = xn[:, c * LANES:(c + 1) * LANES]
        rows1 = PERM_TILE // d1
        rows2 = PERM_TILE // d2
        for base in range(0, ROW_TILE, PERM_TILE):
            for r in range(d1):
                dst = slice(base + r * rows1, base + (r + 1) * rows1)
                for c in range(N_SLABS):
                    blk = xn_ref[c, pl.ds(base + r, rows1, stride=d1), :]
                    s1_ref[c, dst, :] = blk
                    h_ref[1, dst, c * LANES:(c + 1) * LANES] = blk.astype(BF16)
            for r1 in range(d1):
                for r2 in range(d1):
                    r = r1 + d1 * r2
                    dst = slice(base + r * rows2, base + (r + 1) * rows2)
                    for c in range(N_SLABS):
                        blk = s1_ref[c, pl.ds(base + r1 * rows1 + r2, rows2, stride=d1), :]
                        h_ref[2, dst, c * LANES:(c + 1) * LANES] = blk.astype(BF16)

    @pl.when(t == 0)
    def _():
        acc_b_ref[...] = jnp.zeros_like(acc_b_ref)

    def step(acc_w_ref, acc_r_ref, normalise):
        acc_w_ref[...] = jnp.dot(h_ref[j % N_GROUPS], w_ref[...], preferred_element_type=F32)
        prev = acc_r_ref[...]
        if not normalise:
            o_ref[...] = prev.astype(BF16)
            return
        sq = (prev * prev).astype(BF16)
        ms = jnp.concatenate(
            [jnp.dot(sq[:, c:c + MXU_DIM], seg_ref[...], preferred_element_type=F32)
             for c in range(0, COL_TILE, MXU_DIM)], axis=1)
        o_ref[...] = (prev * lax.rsqrt(ms + EPS) * gain_ref[...]).astype(BF16)

    prev_is_qk = (jnp.maximum(t - 1, 0) % n_col_tiles) < n_norm_tiles
    for parity, (acc_w_ref, acc_r_ref) in enumerate(((acc_a_ref, acc_b_ref), (acc_b_ref, acc_a_ref))):
        for normalise in (True, False):
            @pl.when(jnp.logical_and(t % 2 == parity, prev_is_qk == normalise))
            def _(acc_w_ref=acc_w_ref, acc_r_ref=acc_r_ref, normalise=normalise):
                step(acc_w_ref, acc_r_ref, normalise)


def _attn_proj(x, g, w, gains, seg):
    t, d = x.shape
    n = w.shape[1]
    n_col_tiles = n // COL_TILE
    n_row_tiles = t // ROW_TILE
    n_steps = n_row_tiles * n_col_tiles + 1
    n_norm_tiles = 2 * N_GROUPS * ATTN_HEADS * ATTN_HEAD_DIM // COL_TILE

    prev = lambda s: jnp.maximum(s - 1, 0)
    prev_col = lambda s: prev(s) % n_col_tiles
    return pl.pallas_call(
        partial(_attn_proj_kernel, n_col_tiles=n_col_tiles, n_norm_tiles=n_norm_tiles, n_steps=n_steps),
        grid=(n_steps,),
        in_specs=[
            pl.BlockSpec((ROW_TILE, d), lambda s: (jnp.minimum(s // n_col_tiles, n_row_tiles - 1), 0)),
            pl.BlockSpec((1, d), lambda s: (0, 0)),
            pl.BlockSpec((d, COL_TILE), lambda s: (0, s % n_col_tiles)),
            pl.BlockSpec((1, COL_TILE), lambda s: (0, prev_col(s))),
            pl.BlockSpec((MXU_DIM, MXU_DIM), lambda s: (0, 0)),
        ],
        out_specs=pl.BlockSpec((ROW_TILE, COL_TILE), lambda s: (prev(s) // n_col_tiles, prev_col(s))),
        out_shape=jax.ShapeDtypeStruct((t, n), BF16),
        scratch_shapes=[pltpu.VMEM((N_SLABS, ROW_TILE, LANES), F32),
                        pltpu.VMEM((N_SLABS, ROW_TILE, LANES), F32),
                        pltpu.VMEM((N_GROUPS, ROW_TILE, d), BF16),
                        pltpu.VMEM((ROW_TILE, COL_TILE), F32),
                        pltpu.VMEM((ROW_TILE, COL_TILE), F32)],
        compiler_params=_params("arbitrary"),
        name="attn_proj",
    )(x, g, w, gains, seg)


def _attn_kernel(q_ref, k_ref, v_ref, bias_ref, o_ref, lse_ref, den_ref, kprev_ref, vprev_ref):
    n = pl.program_id(2)
    gw = ATTN_HEADS * ATTN_HEAD_DIM

    @pl.when(n == 0)
    def _():
        kprev_ref[...] = jnp.zeros_like(kprev_ref)
        vprev_ref[...] = jnp.zeros_like(vprev_ref)

    first = (n == 0).astype(jnp.int32)
    heads_per_pass = MXU_DIM // ATTN_HEAD_DIM
    lane_head = lax.shift_right_logical(
        lax.broadcasted_iota(jnp.int32, (BAND, MXU_DIM), 1), int(math.log2(ATTN_HEAD_DIM)))
    low_half = lax.broadcasted_iota(jnp.int32, (BAND, LANES), 1) < ATTN_HEAD_DIM
    lse_lane = lax.broadcasted_iota(jnp.int32, (BAND, LANES), 1)
    n_blocks = q_ref.shape[0] * q_ref.shape[1] // BAND
    step_rows = n_blocks * BAND
    max_tiles = [jnp.zeros((BAND, LANES), F32) for _ in range(n_blocks)]
    den_tiles = [jnp.ones((BAND, LANES), F32) for _ in range(n_blocks)]

    for c in range(ATTN_HEADS // heads_per_pass):
        cols = slice(c * MXU_DIM, (c + 1) * MXU_DIM)
        q_all = q_ref[:, :, cols].reshape(step_rows, MXU_DIM)
        k_all = k_ref[:, :, cols].reshape(step_rows, MXU_DIM)
        v_all = v_ref[:, :, cols].reshape(step_rows, MXU_DIM)
        k_prev = kprev_ref[:, cols]
        v_prev = vprev_ref[:, cols]
        kprev_ref[:, cols] = k_all[step_rows - BAND:]
        vprev_ref[:, cols] = v_all[step_rows - BAND:]
        outs = []
        for blk in range(n_blocks):
            rows = slice(blk * BAND, (blk + 1) * BAND)
            qq = q_all[rows]
            kk = jnp.concatenate([k_prev, k_all[rows]], axis=0)
            vv = jnp.concatenate([v_prev, v_all[rows]], axis=0)
            k_prev, v_prev = k_all[rows], v_all[rows]
            table = first if blk == 0 else 0
            lhs = jnp.concatenate(
                [jnp.where(lane_head == hh, qq, jnp.zeros_like(qq)) for hh in range(heads_per_pass)], axis=0)
            s_all = lax.dot_general(lhs, kk, (((1,), (1,)), ((), ())), preferred_element_type=F32)
            probs = []
            for hh in range(heads_per_pass):
                h = c * heads_per_pass + hh
                s = s_all[hh * BAND:(hh + 1) * BAND] + bias_ref[table, h]
                m = jnp.max(s, axis=-1, keepdims=True)
                p = jnp.exp2(s - m)
                den = jnp.sum(p, axis=-1, keepdims=True)
                probs.append(p.astype(BF16))
                max_tiles[blk] = jnp.where(lse_lane == h, m, max_tiles[blk])
                den_tiles[blk] = jnp.where(lse_lane == h, den, den_tiles[blk])
            o_all = jnp.dot(jnp.concatenate(probs, axis=0), vv, preferred_element_type=F32)
            halves = []
            for half in range(MXU_DIM // LANES):
                lanes = slice(half * LANES, (half + 1) * LANES)
                h0 = 2 * half
                halves.append(jnp.where(low_half, o_all[h0 * BAND:(h0 + 1) * BAND, lanes],
                                        o_all[(h0 + 1) * BAND:(h0 + 2) * BAND, lanes]))
            outs.append(jnp.concatenate(halves, axis=1).astype(BF16))
        o_ref[:, :, cols] = jnp.concatenate(outs, axis=0).reshape(o_ref.shape[0], o_ref.shape[1], MXU_DIM)

    den = jnp.concatenate(den_tiles, axis=0)
    lse = (jnp.concatenate(max_tiles, axis=0) + jnp.log2(den)) * LN_2
    lse_ref[...] = lse.reshape(lse_ref.shape)
    den_ref[...] = den.reshape(den_ref.shape)


def _attention_group(qkv, bias, group, batch, seq):
    dilation = DILATION_PATTERNS[group][1]
    gw = ATTN_HEADS * ATTN_HEAD_DIM
    tiles_per_seq = seq // PERM_TILE
    rows_per_residue = PERM_TILE // dilation
    n_tiles = batch * tiles_per_seq

    step_rows = ATTN_BLOCKS_PER_STEP * BAND
    if rows_per_residue >= step_rows:
        per = rows_per_residue // step_rows
        lead, rows = 1, step_rows
        row_idx = lambda b, r, n: (b * tiles_per_seq + n // per, r * per + n % per)
    else:
        lead, rows = step_rows // rows_per_residue, rows_per_residue
        row_idx = lambda b, r, n: (b * (tiles_per_seq // lead) + n, r)

    def spec(width, col):
        return pl.BlockSpec((lead, rows, width), lambda b, r, n: (*row_idx(b, r, n), col))

    view = qkv.reshape(n_tiles, PERM_TILE, qkv.shape[1])
    out, lse, den = pl.pallas_call(
        _attn_kernel,
        grid=(batch, dilation, seq // dilation // step_rows),
        in_specs=[spec(gw, group), spec(gw, N_GROUPS + group), spec(gw, 2 * N_GROUPS + group),
                  pl.BlockSpec(bias.shape, lambda b, r, n: (0, 0, 0, 0))],
        out_specs=[spec(gw, 0), spec(LANES, 0), spec(LANES, 0)],
        out_shape=[jax.ShapeDtypeStruct((n_tiles, PERM_TILE, gw), BF16),
                   jax.ShapeDtypeStruct((n_tiles, PERM_TILE, LANES), F32),
                   jax.ShapeDtypeStruct((n_tiles, PERM_TILE, LANES), F32)],
        scratch_shapes=[pltpu.VMEM((BAND, gw), BF16), pltpu.VMEM((BAND, gw), BF16)],
        compiler_params=_params("arbitrary", "arbitrary", "arbitrary"),
        name=f"attn_group{group}",
    )(view, view, view, bias)
    return (out.reshape(batch * seq, gw), lse.reshape(batch * seq, LANES),
            den.reshape(batch * seq, LANES))


def _t5_bucket(dist):
    max_exact = REL_BUCKETS // 2
    d = jnp.maximum(dist.astype(F32), 1.0)
    large = max_exact + (jnp.log(d / max_exact) / math.log(REL_MAX_DISTANCE / max_exact)
                         * (REL_BUCKETS - max_exact)).astype(jnp.int32)
    large = jnp.minimum(large, REL_BUCKETS - 1)
    return jnp.where(dist < max_exact, dist, large)


def _bias_table(rel_bias_g, dilation):
    qi = jnp.arange(BAND)[:, None]
    ki = jnp.arange(2 * BAND)[None, :]
    steps = qi - ki + BAND
    valid = (steps >= 0) & (steps <= BAND)
    bucket = _t5_bucket(jnp.clip(steps, 0, BAND) * dilation)
    onehot = bucket[None] == jnp.arange(REL_BUCKETS)[:, None, None]
    bias = jnp.sum(jnp.where(onehot[:, None], rel_bias_g.astype(F32)[:, :, None, None], 0.0), axis=0)
    bias = bias * LOG2_E
    regular = jnp.where(valid[None], bias, MASKED)
    first = jnp.where((valid & (ki >= BAND))[None], bias, MASKED)
    return jnp.stack([regular, first])


def _attn_out_kernel(x_ref, o0_ref, o1_ref, o2_ref, l0_ref, l1_ref, l2_ref, d0_ref, d1_ref, d2_ref,
                     expand_ref, w_ref, out_ref, o_scr, l_scr, d_scr):
    lses, dens, outs = [], [], []
    group_refs = ((o0_ref, l0_ref, d0_ref), (o1_ref, l1_ref, d1_ref), (o2_ref, l2_ref, d2_ref))
    for g, (o_ref, l_ref, d_ref) in enumerate(group_refs):
        dil = DILATION_PATTERNS[g][1]
        if dil == 1:
            outs.append(o_ref[...].astype(F32))
            lses.append(l_ref[...])
            dens.append(d_ref[...])
            continue
        rows = PERM_TILE // dil
        for r in range(dil):
            src = slice(r * rows, (r + 1) * rows)
            o_res = o_ref[src, :].astype(F32)
            for c in range(N_SLABS):
                o_scr[g - 1, c, pl.ds(r, rows, stride=dil), :] = o_res[:, c * LANES:(c + 1) * LANES]
            l_scr[g - 1, pl.ds(r, rows, stride=dil), :] = l_ref[src, :]
            d_scr[g - 1, pl.ds(r, rows, stride=dil), :] = d_ref[src, :]
        outs.append(jnp.concatenate([o_scr[g - 1, c] for c in range(N_SLABS)], axis=1))
        lses.append(l_scr[g - 1])
        dens.append(d_scr[g - 1])
    m = jnp.maximum(jnp.maximum(lses[0], lses[1]), lses[2])
    es = [jnp.exp(l - m) for l in lses]
    inv = 1.0 / (es[0] + es[1] + es[2])
    y = None
    for g, e in enumerate(es):
        wt = e * inv / dens[g]
        hi = wt.astype(BF16)
        lo = (wt - hi.astype(F32)).astype(BF16)
        wide = jnp.dot(jnp.concatenate([hi, lo], axis=1), expand_ref[...], preferred_element_type=F32)
        term = wide * outs[g]
        y = term if y is None else y + term
    out_ref[...] = x_ref[...] + jnp.dot(y.astype(BF16), w_ref[...], preferred_element_type=F32)


def _attn_out(x, outs, lses, dens, expand, w):
    t, d = x.shape
    rows = PERM_TILE
    row_spec = lambda width: pl.BlockSpec((rows, width), lambda i: (i, 0))
    const_spec = lambda shape: pl.BlockSpec(shape, lambda i: (0, 0))
    return pl.pallas_call(
        _attn_out_kernel,
        grid=(t // rows,),
        in_specs=[row_spec(d)] + [row_spec(d)] * N_GROUPS + [row_spec(LANES)] * (2 * N_GROUPS)
                 + [const_spec(expand.shape), const_spec(w.shape)],
        out_specs=row_spec(d),
        out_shape=jax.ShapeDtypeStruct((t, d), F32),
        scratch_shapes=[pltpu.VMEM((N_GROUPS - 1, N_SLABS, rows, LANES), F32),
                        pltpu.VMEM((N_GROUPS - 1, rows, LANES), F32),
                        pltpu.VMEM((N_GROUPS - 1, rows, LANES), F32)],
        compiler_params=_params("parallel"),
        name="attn_out",
    )(x, *outs, *lses, *dens, expand, w)


def _mlstm_proj_kernel(x_ref, g_ref, w_ref, wg_ref, gb_ref, o_ref, gates_ref, h_ref):
    j = pl.program_id(1)

    @pl.when(j == 0)
    def _():
        h = _rms_rows(x_ref[...], g_ref[...]).astype(BF16)
        h_ref[...] = h
        gates_ref[...] = jnp.dot(h, wg_ref[...], preferred_element_type=F32) + gb_ref[...]

    o_ref[...] = jnp.dot(h_ref[...], w_ref[...].astype(BF16), preferred_element_type=F32).astype(o_ref.dtype)


def _mlstm_proj(x, g, w_all, layer, n, wg, gb):
    t, d = x.shape
    return pl.pallas_call(
        _mlstm_proj_kernel,
        grid=(t // ROW_TILE, n // COL_TILE),
        in_specs=[
            pl.BlockSpec((ROW_TILE, d), lambda i, j: (i, 0)),
            pl.BlockSpec((1, d), lambda i, j: (0, 0)),
            pl.BlockSpec((None, d, COL_TILE), lambda i, j: (layer, 0, j)),
            pl.BlockSpec((d, LANES), lambda i, j: (0, 0)),
            pl.BlockSpec((1, LANES), lambda i, j: (0, 0)),
        ],
        out_specs=[pl.BlockSpec((ROW_TILE, COL_TILE), lambda i, j: (i, j)),
                   pl.BlockSpec((ROW_TILE, LANES), lambda i, j: (i, 0))],
        out_shape=[jax.ShapeDtypeStruct((t, n), BF16), jax.ShapeDtypeStruct((t, LANES), F32)],
        scratch_shapes=[pltpu.VMEM((ROW_TILE, d), BF16)],
        compiler_params=_params("parallel", "arbitrary"),
        name="mlstm_proj",
    )(x, g, w_all, wg, gb)


def _mlstm_kernel(qk_ref, v_ref, op_ref, gates_ref, cw_ref, cb_ref, og_ref, y_ref,
                  c_scr, n_scr, m_scr, tail_scr):
    chunk = pl.program_id(1)
    L = MLSTM_CHUNK

    @pl.when(chunk == 0)
    def _():
        c_scr[...] = jnp.zeros_like(c_scr)
        n_scr[...] = jnp.zeros_like(n_scr)
        m_scr[...] = jnp.zeros_like(m_scr)
        tail_scr[...] = jnp.zeros_like(tail_scr)

    cw = cw_ref[...]
    lane_id = lax.broadcasted_iota(jnp.int32, (L, LANES), 1)
    row_id = lax.broadcasted_iota(jnp.int32, (LANES, L), 0)
    t_id = lax.broadcasted_iota(jnp.int32, (L, L), 0)
    s_id = lax.broadcasted_iota(jnp.int32, (L, L), 1)
    causal = s_id <= t_id
    qk_width = MLSTM_HEADS * MLSTM_QK_DIM

    for bi in range(qk_ref.shape[0]):
        xqk = qk_ref[bi].astype(F32)
        ext = jnp.concatenate([tail_scr[bi], xqk], axis=0)
        conv = xqk * cw[CONV_WIDTH - 1:CONV_WIDTH] + cb_ref[...]
        for back in range(1, CONV_WIDTH):
            shifted = pltpu.roll(ext, back, axis=0)[SUBLANES:]
            conv = conv + shifted * cw[CONV_WIDTH - 1 - back:CONV_WIDTH - back]
        tail_scr[bi] = xqk[L - SUBLANES:]
        act = conv * _sigmoid(conv)

        gates = gates_ref[bi]
        log_f = _log_sigmoid(gates)
        gates_t = gates.T
        log_f_t = log_f.T

        for h in range(MLSTM_HEADS):
            slot = bi * MLSTM_HEADS + h
            q = act[:, h * MLSTM_QK_DIM:(h + 1) * MLSTM_QK_DIM] * (MLSTM_QK_DIM ** -0.5)
            k = act[:, qk_width + h * MLSTM_QK_DIM:qk_width + (h + 1) * MLSTM_QK_DIM]
            vcols = slice(h * MLSTM_V_DIM, (h + 1) * MLSTM_V_DIM)
            v = v_ref[bi, :, vcols]
            i_col = jnp.sum(jnp.where(lane_id == h, gates, 0.0), axis=1, keepdims=True)
            f_col = jnp.sum(jnp.where(lane_id == MLSTM_HEADS + h, log_f, 0.0), axis=1, keepdims=True)
            i_row = jnp.sum(jnp.where(row_id == h, gates_t, 0.0), axis=0, keepdims=True)
            f_row = jnp.sum(jnp.where(row_id == MLSTM_HEADS + h, log_f_t, 0.0), axis=0, keepdims=True)
            bcum_col = jnp.sum(jnp.where(causal, f_row, 0.0), axis=1, keepdims=True)
            bcum_row = jnp.sum(jnp.where(t_id <= s_id, f_col, 0.0), axis=0, keepdims=True)
            b_last = jnp.sum(f_col, axis=0, keepdims=True)
            m_prev = m_scr[slot][0:1, 0:1]
            n_prev = n_scr[slot][0:1, :]
            c_prev = c_scr[slot]

            dmat = jnp.where(causal, bcum_col - bcum_row + i_row, -jnp.inf)
            m_t = jnp.maximum(bcum_col + m_prev, jnp.max(dmat, axis=1, keepdims=True))
            inter = jnp.exp(bcum_col + m_prev - m_t)
            qb = q.astype(BF16)
            kb = k.astype(BF16)
            scores = lax.dot_general(qb, kb, (((1,), (1,)), ((), ())), preferred_element_type=F32)
            wts = scores * jnp.exp(dmat - m_t)
            num = (inter * jnp.dot(qb, c_prev.astype(BF16), preferred_element_type=F32)
                   + jnp.dot(wts.astype(BF16), v, preferred_element_type=F32))
            den = (inter * jnp.sum(q * n_prev, axis=1, keepdims=True)
                   + jnp.sum(wts, axis=1, keepdims=True))
            inv_den = 1.0 / jnp.maximum(jnp.abs(den), jnp.exp(-m_t))
            ms = (inv_den * inv_den) * jnp.mean(num * num, axis=1, keepdims=True)
            row_scale = inv_den * lax.rsqrt(ms + EPS)

            a_col = b_last - bcum_col + i_col
            m_new = jnp.maximum(b_last + m_prev, jnp.max(a_col, axis=0, keepdims=True))
            decay = jnp.exp(b_last + m_prev - m_new)
            kw = jnp.exp(a_col - m_new) * k
            c_scr[slot] = decay * c_prev + jnp.dot(kw.T.astype(BF16), v, preferred_element_type=F32)
            n_new = decay * n_prev + jnp.sum(kw, axis=0, keepdims=True)
            n_scr[slot] = jnp.broadcast_to(n_new, (SUBLANES, MLSTM_QK_DIM))
            m_scr[slot] = jnp.broadcast_to(m_new, (SUBLANES, LANES))

            gated = num * row_scale * og_ref[:, vcols] * _sigmoid(op_ref[bi, :, vcols].astype(F32))
            y_ref[bi, :, vcols] = gated.astype(y_ref.dtype)


def _mlstm_cell(proj, gates, conv_w, conv_b, out_gain, batch, seq):
    L = MLSTM_CHUNK
    nb = MLSTM_BATCH_PER_STEP
    d = D_MODEL
    slots = nb * MLSTM_HEADS
    proj3 = proj.reshape(batch, seq, proj.shape[1])
    col_spec = lambda which: pl.BlockSpec((nb, L, d), lambda b, c: (b, c, which))
    const_spec = lambda shape: pl.BlockSpec(shape, lambda b, c: (0, 0))
    y = pl.pallas_call(
        _mlstm_kernel,
        grid=(batch // nb, seq // L),
        in_specs=[col_spec(0), col_spec(1), col_spec(2),
                  pl.BlockSpec((nb, L, LANES), lambda b, c: (b, c, 0)),
                  const_spec(conv_w.shape), const_spec(conv_b.shape), const_spec(out_gain.shape)],
        out_specs=pl.BlockSpec((nb, L, d), lambda b, c: (b, c, 0)),
        out_shape=jax.ShapeDtypeStruct((batch, seq, d), BF16),
        scratch_shapes=[pltpu.VMEM((slots, MLSTM_QK_DIM, MLSTM_V_DIM), F32),
                        pltpu.VMEM((slots, SUBLANES, MLSTM_QK_DIM), F32),
                        pltpu.VMEM((slots, SUBLANES, LANES), F32),
                        pltpu.VMEM((nb, SUBLANES, d), F32)],
        compiler_params=_params("arbitrary", "arbitrary"),
        name="mlstm_cell",
    )(proj3, proj3, proj3, gates.reshape(batch, seq, LANES), conv_w, conv_b, out_gain)
    return y.reshape(batch * seq, d)


def _residual_matmul_kernel(x_ref, y_ref, w_ref, out_ref):
    out_ref[...] = x_ref[...] + jnp.dot(y_ref[...], w_ref[...], preferred_element_type=F32)


def _residual_matmul(x, y, w):
    t, d = x.shape
    row_spec = pl.BlockSpec((ROW_TILE, d), lambda i: (i, 0))
    return pl.pallas_call(
        _residual_matmul_kernel,
        grid=(t // ROW_TILE,),
        in_specs=[row_spec, row_spec, pl.BlockSpec(w.shape, lambda i: (0, 0))],
        out_specs=row_spec,
        out_shape=jax.ShapeDtypeStruct((t, d), F32),
        compiler_params=_params("parallel"),
        name="residual_matmul",
    )(x, y, w)


def _mlp_kernel(x_ref, g_ref, w1_ref, w2_ref, out_ref, h_ref):
    j = pl.program_id(1)

    @pl.when(j == 0)
    def _():
        x = x_ref[...]
        h_ref[...] = _rms_rows(x, g_ref[...]).astype(BF16)
        out_ref[...] = x

    hid = jnp.maximum(jnp.dot(h_ref[...], w1_ref[...].astype(BF16), preferred_element_type=F32), 0.0)
    out_ref[...] += jnp.dot((hid * hid).astype(BF16), w2_ref[...].astype(BF16), preferred_element_type=F32)


def _mlp(x, g, w1_all, w2_all, layer):
    t, d = x.shape
    f = w1_all.shape[2]
    return pl.pallas_call(
        _mlp_kernel,
        grid=(t // ROW_TILE, f // COL_TILE),
        in_specs=[
            pl.BlockSpec((ROW_TILE, d), lambda i, j: (i, 0)),
            pl.BlockSpec((1, d), lambda i, j: (0, 0)),
            pl.BlockSpec((None, d, COL_TILE), lambda i, j: (layer, 0, j)),
            pl.BlockSpec((None, COL_TILE, d), lambda i, j: (layer, j, 0)),
        ],
        out_specs=pl.BlockSpec((ROW_TILE, d), lambda i, j: (i, 0)),
        out_shape=jax.ShapeDtypeStruct((t, d), F32),
        scratch_shapes=[pltpu.VMEM((ROW_TILE, d), BF16)],
        compiler_params=_params("parallel", "arbitrary"),
        name="mlp",
    )(x, g, w1_all, w2_all)


def _attention_layer(x, batch, seq, norm_g, w_in, q_gain, k_gain, w_out, rel_bias):
    gw = ATTN_HEADS * ATTN_HEAD_DIM
    gains = jnp.concatenate([
        jnp.tile(q_gain.astype(F32) * (ATTN_HEAD_DIM ** -0.5 * LOG2_E), (1, ATTN_HEADS)).reshape(-1),
        jnp.tile(k_gain.astype(F32), (1, ATTN_HEADS)).reshape(-1),
        jnp.ones((N_GROUPS * gw,), F32)])[None, :]
    head_of_lane = jnp.arange(MXU_DIM) // ATTN_HEAD_DIM
    seg = ((head_of_lane[:, None] == head_of_lane[None, :]).astype(F32) * (1.0 / ATTN_HEAD_DIM)).astype(BF16)
    qkv = _attn_proj(x, norm_g[None, :], w_in.astype(BF16), gains, seg)

    outs, lses, dens = [], [], []
    for g, (_, dilation) in enumerate(DILATION_PATTERNS):
        o, l, dn = _attention_group(qkv, _bias_table(rel_bias[:, g], dilation), g, batch, seq)
        outs.append(o)
        lses.append(l)
        dens.append(dn)

    head_rows = jnp.arange(LANES)[:, None]
    head_cols = (jnp.arange(gw) // ATTN_HEAD_DIM)[None, :]
    spread = (head_rows == head_cols).astype(BF16)
    expand = jnp.concatenate([spread, spread], axis=0)
    return _attn_out(x, outs, lses, dens, expand, w_out.astype(BF16))


def _mlstm_layer(x, batch, seq, norm_g, w_in_all, layer, gate_bias, conv_w, conv_b, out_gain, w_out):
    main = 2 * MLSTM_HEADS * MLSTM_QK_DIM + MLSTM_HEADS * MLSTM_V_DIM + D_MODEL
    n_gates = 2 * MLSTM_HEADS
    wg = jnp.pad(w_in_all[layer, :, main:], ((0, 0), (0, LANES - n_gates))).astype(BF16)
    gb = jnp.pad(gate_bias.astype(F32), (0, LANES - n_gates))[None, :]
    proj, gates = _mlstm_proj(x, norm_g[None, :], w_in_all, layer, main, wg, gb)
    y = _mlstm_cell(proj, gates, conv_w.astype(F32), conv_b.astype(F32)[None, :],
                    out_gain.astype(F32).reshape(1, -1), batch, seq)
    return _residual_matmul(x, y, w_out.astype(BF16))


def kernel(x, mixer_norm, mlp_norm, rel_bias, attn_w_in, attn_q_gain, attn_k_gain, attn_w_out,
           mlstm_w_in, mlstm_gate_bias, mlstm_conv_w, mlstm_conv_b, mlstm_out_gain, mlstm_w_out,
           mlp_w_in, mlp_w_out):
    batch, seq, d = x.shape
    depth = mixer_norm.shape[0]
    h = x.reshape(batch * seq, d)
    for layer in range(depth):
        j = layer // 2
        if layer % 2 == 0:
            h = _attention_layer(h, batch, seq, mixer_norm[layer], attn_w_in[j], attn_q_gain[j],
                                 attn_k_gain[j], attn_w_out[j], rel_bias)
        else:
            h = _mlstm_layer(h, batch, seq, mixer_norm[layer], mlstm_w_in, j, mlstm_gate_bias[j],
                             mlstm_conv_w[j], mlstm_conv_b[j], mlstm_out_gain[j], mlstm_w_out[j])
        h = _mlp(h, mlp_norm[layer][None, :], mlp_w_in, mlp_w_out, layer)
    return h.reshape(batch, seq, d)
```

```python
import math
from functools import partial

import jax
import jax.numpy as jnp
from jax import lax
from jax.experimental import pallas as pl
from jax.experimental.pallas import tpu as pltpu

F32 = jnp.float32
BF16 = jnp.bfloat16

D_MODEL = 1024
EPS = 1e-6
LOG2_E = math.log2(math.e)
LN_2 = math.log(2.0)

ATTN_HEADS = 16
ATTN_HEAD_DIM = 64
DILATION_PATTERNS = ((128, 1), (512, 4), (2048, 16))
N_GROUPS = len(DILATION_PATTERNS)
BAND = 128
REL_BUCKETS = 32
REL_MAX_DISTANCE = 2048
MASKED = -1e30
PERM_TILE = 512
ATTN_BLOCKS_PER_STEP = 4

MLSTM_HEADS = 4
MLSTM_V_DIM = 256
MLSTM_QK_DIM = 128
CONV_WIDTH = 4
MLSTM_CHUNK = 256
MLSTM_BATCH_PER_STEP = 1

LANES = 128
SUBLANES = 8
MXU_DIM = 256
VMEM_LIMIT_BYTES = 56 * 1024 * 1024

ROW_TILE = 1024
COL_TILE = 1024
N_SLABS = D_MODEL // LANES


def _params(*semantics):
    return pltpu.CompilerParams(dimension_semantics=semantics, vmem_limit_bytes=VMEM_LIMIT_BYTES)


def _rms_rows(x, gain):
    ms = jnp.mean(x * x, axis=-1, keepdims=True)
    return x * lax.rsqrt(ms + EPS) * gain


def _sigmoid(x):
    return 0.5 * jnp.tanh(0.5 * x) + 0.5


def _log_sigmoid(x):
    return jnp.minimum(x, 0.0) - jnp.log1p(jnp.exp(-jnp.abs(x)))


def _attn_proj_kernel(x_ref, g_ref, w_ref, gain_ref, seg_ref, o_ref,
                      xn_ref, s1_ref, h_ref, acc_a_ref, acc_b_ref, *, n_col_tiles, n_norm_tiles, n_steps):
    t = pl.program_id(0)
    j = t % n_col_tiles
    d1 = DILATION_PATTERNS[1][1]
    d2 = DILATION_PATTERNS[2][1]
    assert DILATION_PATTERNS[0][1] == 1 and d2 == d1 * d1

    @pl.when(jnp.logical_and(j == 0, t < n_steps - 1))
    def _():
        xn = _rms_rows(x_ref[...], g_ref[...])
        h_ref[0] = xn.astype(BF16)
        for c in range(N_SLABS):
            xn_ref[c] = xn[:, c * LANES:(c + 1) * LANES]
        rows1 = PERM_TILE // d1
        rows2 = PERM_TILE // d2
        for base in range(0, ROW_TILE, PERM_TILE):
            for r in range(d1):
                dst = slice(base + r * rows1, base + (r + 1) * rows1)
                for c in range(N_SLABS):
                    blk = xn_ref[c, pl.ds(base + r, rows1, stride=d1), :]
                    s1_ref[c, dst, :] = blk
                    h_ref[1, dst, c * LANES:(c + 1) * LANES] = blk.astype(BF16)
            for r1 in range(d1):
                for r2 in range(d1):
                    r = r1 + d1 * r2
                    dst = slice(base + r * rows2, base + (r + 1) * rows2)
                    for c in range(N_SLABS):
                        blk = s1_ref[c, pl.ds(base + r1 * rows1 + r2, rows2, stride=d1), :]
                        h_ref[2, dst, c * LANES:(c + 1) * LANES] = blk.astype(BF16)

    @pl.when(t == 0)
    def _():
        acc_b_ref[...] = jnp.zeros_like(acc_b_ref)

    def step(acc_w_ref, acc_r_ref, normalise):
        acc_w_ref[...] = jnp.dot(h_ref[j % N_GROUPS], w_ref[...], preferred_element_type=F32)
        prev = acc_r_ref[...]
        if not normalise:
            o_ref[...] = prev.astype(BF16)
            return
        sq = (prev * prev).astype(BF16)
        ms = jnp.concatenate(
            [jnp.dot(sq[:, c:c + MXU_DIM], seg_ref[...], preferred_element_type=F32)
             for c in range(0, COL_TILE, MXU_DIM)], axis=1)
        o_ref[...] = (prev * lax.rsqrt(ms + EPS) * gain_ref[...]).astype(BF16)

    prev_is_qk = (jnp.maximum(t - 1, 0) % n_col_tiles) < n_norm_tiles
    for parity, (acc_w_ref, acc_r_ref) in enumerate(((acc_a_ref, acc_b_ref), (acc_b_ref, acc_a_ref))):
        for normalise in (True, False):
            @pl.when(jnp.logical_and(t % 2 == parity, prev_is_qk == normalise))
            def _(acc_w_ref=acc_w_ref, acc_r_ref=acc_r_ref, normalise=normalise):
                step(acc_w_ref, acc_r_ref, normalise)


def _attn_proj(x, g, w, gains, seg):
    t, d = x.shape
    n = w.shape[1]
    n_col_tiles = n // COL_TILE
    n_row_tiles = t // ROW_TILE
    n_steps = n_row_tiles * n_col_tiles + 1
    n_norm_tiles = 2 * N_GROUPS * ATTN_HEADS * ATTN_HEAD_DIM // COL_TILE

    prev = lambda s: jnp.maximum(s - 1, 0)
    prev_col = lambda s: prev(s) % n_col_tiles
    return pl.pallas_call(
        partial(_attn_proj_kernel, n_col_tiles=n_col_tiles, n_norm_tiles=n_norm_tiles, n_steps=n_steps),
        grid=(n_steps,),
        in_specs=[
            pl.BlockSpec((ROW_TILE, d), lambda s: (jnp.minimum(s // n_col_tiles, n_row_tiles - 1), 0)),
            pl.BlockSpec((1, d), lambda s: (0, 0)),
            pl.BlockSpec((d, COL_TILE), lambda s: (0, s % n_col_tiles)),
            pl.BlockSpec((1, COL_TILE), lambda s: (0, prev_col(s))),
            pl.BlockSpec((MXU_DIM, MXU_DIM), lambda s: (0, 0)),
        ],
        out_specs=pl.BlockSpec((ROW_TILE, COL_TILE), lambda s: (prev(s) // n_col_tiles, prev_col(s))),
        out_shape=jax.ShapeDtypeStruct((t, n), BF16),
        scratch_shapes=[pltpu.VMEM((N_SLABS, ROW_TILE, LANES), F32),
                        pltpu.VMEM((N_SLABS, ROW_TILE, LANES), F32),
                        pltpu.VMEM((N_GROUPS, ROW_TILE, d), BF16),
                        pltpu.VMEM((ROW_TILE, COL_TILE), F32),
                        pltpu.VMEM((ROW_TILE, COL_TILE), F32)],
        compiler_params=_params("arbitrary"),
        name="attn_proj",
    )(x, g, w, gains, seg)


def _attn_kernel(q_ref, k_ref, v_ref, bias_ref, o_ref, lse_ref, den_ref, kprev_ref, vprev_ref):
    n = pl.program_id(2)
    gw = ATTN_HEADS * ATTN_HEAD_DIM

    @pl.when(n == 0)
    def _():
        kprev_ref[...] = jnp.zeros_like(kprev_ref)
        vprev_ref[...] = jnp.zeros_like(vprev_ref)

    first = (n == 0).astype(jnp.int32)
    heads_per_pass = MXU_DIM // ATTN_HEAD_DIM
    lane_head = lax.shift_right_logical(
        lax.broadcasted_iota(jnp.int32, (BAND, MXU_DIM), 1), int(math.log2(ATTN_HEAD_DIM)))
    low_half = lax.broadcasted_iota(jnp.int32, (BAND, LANES), 1) < ATTN_HEAD_DIM
    lse_lane = lax.broadcasted_iota(jnp.int32, (BAND, LANES), 1)
    n_blocks = q_ref.shape[0] * q_ref.shape[1] // BAND
    step_rows = n_blocks * BAND
    max_tiles = [jnp.zeros((BAND, LANES), F32) for _ in range(n_blocks)]
    den_tiles = [jnp.ones((BAND, LANES), F32) for _ in range(n_blocks)]

    for c in range(ATTN_HEADS // heads_per_pass):
        cols = slice(c * MXU_DIM, (c + 1) * MXU_DIM)
        q_all = q_ref[:, :, cols].reshape(step_rows, MXU_DIM)
        k_all = k_ref[:, :, cols].reshape(step_rows, MXU_DIM)
        v_all = v_ref[:, :, cols].reshape(step_rows, MXU_DIM)
        k_prev = kprev_ref[:, cols]
        v_prev = vprev_ref[:, cols]
        kprev_ref[:, cols] = k_all[step_rows - BAND:]
        vprev_ref[:, cols] = v_all[step_rows - BAND:]
        outs = []
        for blk in range(n_blocks):
            rows = slice(blk * BAND, (blk + 1) * BAND)
            qq = q_all[rows]
            kk = jnp.concatenate([k_prev, k_all[rows]], axis=0)
            vv = jnp.concatenate([v_prev, v_all[rows]], axis=0)
            k_prev, v_prev = k_all[rows], v_all[rows]
            table = first if blk == 0 else 0
            lhs = jnp.concatenate(
                [jnp.where(lane_head == hh, qq, jnp.zeros_like(qq)) for hh in range(heads_per_pass)], axis=0)
            s_all = lax.dot_general(lhs, kk, (((1,), (1,)), ((), ())), preferred_element_type=F32)
            probs = []
            for hh in range(heads_per_pass):
                h = c * heads_per_pass + hh
                s = s_all[hh * BAND:(hh + 1) * BAND] + bias_ref[table, h]
                m = jnp.max(s, axis=-1, keepdims=True)
                p = jnp.exp2(s - m)
                den = jnp.sum(p, axis=-1, keepdims=True)
                probs.append(p.astype(BF16))
                max_tiles[blk] = jnp.where(lse_lane == h, m, max_tiles[blk])
                den_tiles[blk] = jnp.where(lse_lane == h, den, den_tiles[blk])
            o_all = jnp.dot(jnp.concatenate(probs, axis=0), vv, preferred_element_type=F32)
            halves = []
            for half in range(MXU_DIM // LANES):
                lanes = slice(half * LANES, (half + 1) * LANES)
                h0 = 2 * half
                halves.append(jnp.where(low_half, o_all[h0 * BAND:(h0 + 1) * BAND, lanes],
                                        o_all[(h0 + 1) * BAND:(h0 + 2) * BAND, lanes]))
            outs.append(jnp.concatenate(halves, axis=1).astype(BF16))
        o_ref[:, :, cols] = jnp.concatenate(outs, axis=0).reshape(o_ref.shape[0], o_ref.shape[1], MXU_DIM)

    den = jnp.concatenate(den_tiles, axis=0)
    lse = (jnp.concatenate(max_tiles, axis=0) + jnp.log2(den)) * LN_2
    lse_ref[...] = lse.reshape(lse_ref.shape)
    den_ref[...] = den.reshape(den_ref.shape)


def _attention_group(qkv, bias, group, batch, seq):
    dilation = DILATION_PATTERNS[group][1]
    gw = ATTN_HEADS * ATTN_HEAD_DIM
    tiles_per_seq = seq // PERM_TILE
    rows_per_residue = PERM_TILE // dilation
    n_tiles = batch * tiles_per_seq

    step_rows = ATTN_BLOCKS_PER_STEP * BAND
    if rows_per_residue >= step_rows:
        per = rows_per_residue // step_rows
        lead, rows = 1, step_rows
        row_idx = lambda b, r, n: (b * tiles_per_seq + n // per, r * per + n % per)
    else:
        lead, rows = step_rows // rows_per_residue, rows_per_residue
        row_idx = lambda b, r, n: (b * (tiles_per_seq // lead) + n, r)

    def spec(width, col):
        return pl.BlockSpec((lead, rows, width), lambda b, r, n: (*row_idx(b, r, n), col))

    view = qkv.reshape(n_tiles, PERM_TILE, qkv.shape[1])
    out, lse, den = pl.pallas_call(
        _attn_kernel,
        grid=(batch, dilation, seq // dilation // step_rows),
        in_specs=[spec(gw, group), spec(gw, N_GROUPS + group), spec(gw, 2 * N_GROUPS + group),
                  pl.BlockSpec(bias.shape, lambda b, r, n: (0, 0, 0, 0))],
        out_specs=[spec(gw, 0), spec(LANES, 0), spec(LANES, 0)],
        out_shape=[jax.ShapeDtypeStruct((n_tiles, PERM_TILE, gw), BF16),
                   jax.ShapeDtypeStruct((n_tiles, PERM_TILE, LANES), F32),
                   jax.ShapeDtypeStruct((n_tiles, PERM_TILE, LANES), F32)],
        scratch_shapes=[pltpu.VMEM((BAND, gw), BF16), pltpu.VMEM((BAND, gw), BF16)],
        compiler_params=_params("arbitrary", "arbitrary", "arbitrary"),
        name=f"attn_group{group}",
    )(view, view, view, bias)
    return (out.reshape(batch * seq, gw), lse.reshape(batch * seq, LANES),
            den.reshape(batch * seq, LANES))


def _t5_bucket(dist):
    max_exact = REL_BUCKETS // 2
    d = jnp.maximum(dist.astype(F32), 1.0)
    large = max_exact + (jnp.log(d / max_exact) / math.log(REL_MAX_DISTANCE / max_exact)
                         * (REL_BUCKETS - max_exact)).astype(jnp.int32)
    large = jnp.minimum(large, REL_BUCKETS - 1)
    return jnp.where(dist < max_exact, dist, large)


def _bias_table(rel_bias_g, dilation):
    qi = jnp.arange(BAND)[:, None]
    ki = jnp.arange(2 * BAND)[None, :]
    steps = qi - ki + BAND
    valid = (steps >= 0) & (steps <= BAND)
    bucket = _t5_bucket(jnp.clip(steps, 0, BAND) * dilation)
    onehot = bucket[None] == jnp.arange(REL_BUCKETS)[:, None, None]
    bias = jnp.sum(jnp.where(onehot[:, None], rel_bias_g.astype(F32)[:, :, None, None], 0.0), axis=0)
    bias = bias * LOG2_E
    regular = jnp.where(valid[None], bias, MASKED)
    first = jnp.where((valid & (ki >= BAND))[None], bias, MASKED)
    return jnp.stack([regular, first])


def _attn_out_kernel(x_ref, o0_ref, o1_ref, o2_ref, l0_ref, l1_ref, l2_ref, d0_ref, d1_ref, d2_ref,
                     expand_ref, w_ref, out_ref, o_scr, l_scr, d_scr):
    lses, dens, outs = [], [], []
    group_refs = ((o0_ref, l0_ref, d0_ref), (o1_ref, l1_ref, d1_ref), (o2_ref, l2_ref, d2_ref))
    for g, (o_ref, l_ref, d_ref) in enumerate(group_refs):
        dil = DILATION_PATTERNS[g][1]
        if dil == 1:
            outs.append(o_ref[...].astype(F32))
            lses.append(l_ref[...])
            dens.append(d_ref[...])
            continue
        rows = PERM_TILE // dil
        for r in range(dil):
            src = slice(r * rows, (r + 1) * rows)
            o_res = o_ref[src, :].astype(F32)
            for c in range(N_SLABS):
                o_scr[g - 1, c, pl.ds(r, rows, stride=dil), :] = o_res[:, c * LANES:(c + 1) * LANES]
            l_scr[g - 1, pl.ds(r, rows, stride=dil), :] = l_ref[src, :]
            d_scr[g - 1, pl.ds(r, rows, stride=dil), :] = d_ref[src, :]
        outs.append(jnp.concatenate([o_scr[g - 1, c] for c in range(N_SLABS)], axis=1))
        lses.append(l_scr[g - 1])
        dens.append(d_scr[g - 1])
    m = jnp.maximum(jnp.maximum(lses[0], lses[1]), lses[2])
    es = [jnp.exp(l - m) for l in lses]
    inv = 1.0 / (es[0] + es[1] + es[2])
    y = None
    for g, e in enumerate(es):
        wt = e * inv / dens[g]
        hi = wt.astype(BF16)
        lo = (wt - hi.astype(F32)).astype(BF16)
        wide = jnp.dot(jnp.concatenate([hi, lo], axis=1), expand_ref[...], preferred_element_type=F32)
        term = wide * outs[g]
        y = term if y is None else y + term
    out_ref[...] = x_ref[...] + jnp.dot(y.astype(BF16), w_ref[...], preferred_element_type=F32)


def _attn_out(x, outs, lses, dens, expand, w):
    t, d = x.shape
    rows = PERM_TILE
    row_spec = lambda width: pl.BlockSpec((rows, width), lambda i: (i, 0))
    const_spec = lambda shape: pl.BlockSpec(shape, lambda i: (0, 0))
    return pl.pallas_call(
        _attn_out_kernel,
        grid=(t // rows,),
        in_specs=[row_spec(d)] + [row_spec(d)] * N_GROUPS + [row_spec(LANES)] * (2 * N_GROUPS)
                 + [const_spec(expand.shape), const_spec(w.shape)],
        out_specs=row_spec(d),
        out_shape=jax.ShapeDtypeStruct((t, d), F32),
        scratch_shapes=[pltpu.VMEM((N_GROUPS - 1, N_SLABS, rows, LANES), F32),
                        pltpu.VMEM((N_GROUPS - 1, rows, LANES), F32),
                        pltpu.VMEM((N_GROUPS - 1, rows, LANES), F32)],
        compiler_params=_params("parallel"),
        name="attn_out",
    )(x, *outs, *lses, *dens, expand, w)


def _mlstm_proj_kernel(x_ref, g_ref, w_ref, wg_ref, gb_ref, o_ref, gates_ref, h_ref):
    j = pl.program_id(1)

    @pl.when(j == 0)
    def _():
        h = _rms_rows(x_ref[...], g_ref[...]).astype(BF16)
        h_ref[...] = h
        gates_ref[...] = jnp.dot(h, wg_ref[...], preferred_element_type=F32) + gb_ref[...]

    o_ref[...] = jnp.dot(h_ref[...], w_ref[...].astype(BF16), preferred_element_type=F32).astype(o_ref.dtype)


def _mlstm_proj(x, g, w_all, layer, n, wg, gb):
    t, d = x.shape
    return pl.pallas_call(
        _mlstm_proj_kernel,
        grid=(t // ROW_TILE, n // COL_TILE),
        in_specs=[
            pl.BlockSpec((ROW_TILE, d), lambda i, j: (i, 0)),
            pl.BlockSpec((1, d), lambda i, j: (0, 0)),
            pl.BlockSpec((None, d, COL_TILE), lambda i, j: (layer, 0, j)),
            pl.BlockSpec((d, LANES), lambda i, j: (0, 0)),
            pl.BlockSpec((1, LANES), lambda i, j: (0, 0)),
        ],
        out_specs=[pl.BlockSpec((ROW_TILE, COL_TILE), lambda i, j: (i, j)),
                   pl.BlockSpec((ROW_TILE, LANES), lambda i, j: (i, 0))],
        out_shape=[jax.ShapeDtypeStruct((t, n), BF16), jax.ShapeDtypeStruct((t, LANES), F32)],
        scratch_shapes=[pltpu.VMEM((ROW_TILE, d), BF16)],
        compiler_params=_params("parallel", "arbitrary"),
        name="mlstm_proj",
    )(x, g, w_all, wg, gb)


def _mlstm_kernel(qk_ref, v_ref, op_ref, gates_ref, cw_ref, cb_ref, og_ref, y_ref,
                  c_scr, n_scr, m_scr, tail_scr):
    chunk = pl.program_id(1)
    L = MLSTM_CHUNK

    @pl.when(chunk == 0)
    def _():
        c_scr[...] = jnp.zeros_like(c_scr)
        n_scr[...] = jnp.zeros_like(n_scr)
        m_scr[...] = jnp.zeros_like(m_scr)
        tail_scr[...] = jnp.zeros_like(tail_scr)

    cw = cw_ref[...]
    lane_id = lax.broadcasted_iota(jnp.int32, (L, LANES), 1)
    row_id = lax.broadcasted_iota(jnp.int32, (LANES, L), 0)
    t_id = lax.broadcasted_iota(jnp.int32, (L, L), 0)
    s_id = lax.broadcasted_iota(jnp.int32, (L, L), 1)
    causal = s_id <= t_id
    qk_width = MLSTM_HEADS * MLSTM_QK_DIM

    for bi in range(qk_ref.shape[0]):
        xqk = qk_ref[bi].astype(F32)
        ext = jnp.concatenate([tail_scr[bi], xqk], axis=0)
        conv = xqk * cw[CONV_WIDTH - 1:CONV_WIDTH] + cb_ref[...]
        for back in range(1, CONV_WIDTH):
            shifted = pltpu.roll(ext, back, axis=0)[SUBLANES:]
            conv = conv + shifted * cw[CONV_WIDTH - 1 - back:CONV_WIDTH - back]
        tail_scr[bi] = xqk[L - SUBLANES:]
        act = conv * _sigmoid(conv)

        gates = gates_ref[bi]
        log_f = _log_sigmoid(gates)
        gates_t = gates.T
        log_f_t = log_f.T

        for h in range(MLSTM_HEADS):
            slot = bi * MLSTM_HEADS + h
            q = act[:, h * MLSTM_QK_DIM:(h + 1) * MLSTM_QK_DIM] * (MLSTM_QK_DIM ** -0.5)
            k = act[:, qk_width + h * MLSTM_QK_DIM:qk_width + (h + 1) * MLSTM_QK_DIM]
            vcols = slice(h * MLSTM_V_DIM, (h + 1) * MLSTM_V_DIM)
            v = v_ref[bi, :, vcols]
            i_col = jnp.sum(jnp.where(lane_id == h, gates, 0.0), axis=1, keepdims=True)
            f_col = jnp.sum(jnp.where(lane_id == MLSTM_HEADS + h, log_f, 0.0), axis=1, keepdims=True)
            i_row = jnp.sum(jnp.where(row_id == h, gates_t, 0.0), axis=0, keepdims=True)
            f_row = jnp.sum(jnp.where(row_id == MLSTM_HEADS + h, log_f_t, 0.0), axis=0, keepdims=True)
            bcum_col = jnp.sum(jnp.where(causal, f_row, 0.0), axis=1, keepdims=True)
            bcum_row = jnp.sum(jnp.where(t_id <= s_id, f_col, 0.0), axis=0, keepdims=True)
            b_last = jnp.sum(f_col, axis=0, keepdims=True)
            m_prev = m_scr[slot][0:1, 0:1]
            n_prev = n_scr[slot][0:1, :]
            c_prev = c_scr[slot]

            dmat = jnp.where(causal, bcum_col - bcum_row + i_row, -jnp.inf)
            m_t = jnp.maximum(bcum_col + m_prev, jnp.max(dmat, axis=1, keepdims=True))
            inter = jnp.exp(bcum_col + m_prev - m_t)
            qb = q.astype(BF16)
            kb = k.astype(BF16)
            scores = lax.dot_general(qb, kb, (((1,), (1,)), ((), ())), preferred_element_type=F32)
            wts = scores * jnp.exp(dmat - m_t)
            num = (inter * jnp.dot(qb, c_prev.astype(BF16), preferred_element_type=F32)
                   + jnp.dot(wts.astype(BF16), v, preferred_element_type=F32))
            den = (inter * jnp.sum(q * n_prev, axis=1, keepdims=True)
                   + jnp.sum(wts, axis=1, keepdims=True))
            inv_den = 1.0 / jnp.maximum(jnp.abs(den), jnp.exp(-m_t))
            ms = (inv_den * inv_den) * jnp.mean(num * num, axis=1, keepdims=True)
            row_scale = inv_den * lax.rsqrt(ms + EPS)

            a_col = b_last - bcum_col + i_col
            m_new = jnp.maximum(b_last + m_prev, jnp.max(a_col, axis=0, keepdims=True))
            decay = jnp.exp(b_last + m_prev - m_new)
            kw = jnp.exp(a_col - m_new) * k
            c_scr[slot] = decay * c_prev + jnp.dot(kw.T.astype(BF16), v, preferred_element_type=F32)
            n_new = decay * n_prev + jnp.sum(kw, axis=0, keepdims=True)
            n_scr[slot] = jnp.broadcast_to(n_new, (SUBLANES, MLSTM_QK_DIM))
            m_scr[slot] = jnp.broadcast_to(m_new, (SUBLANES, LANES))

            gated = num * row_scale * og_ref[:, vcols] * _sigmoid(op_ref[bi, :, vcols].astype(F32))
            y_ref[bi, :, vcols] = gated.astype(y_ref.dtype)


def _mlstm_cell(proj, gates, conv_w, conv_b, out_gain, batch, seq):
    L = MLSTM_CHUNK
    nb = MLSTM_BATCH_PER_STEP
    d = D_MODEL
    slots = nb * MLSTM_HEADS
    proj3 = proj.reshape(batch, seq, proj.shape[1])
    col_spec = lambda which: pl.BlockSpec((nb, L, d), lambda b, c: (b, c, which))
    const_spec = lambda shape: pl.BlockSpec(shape, lambda b, c: (0, 0))
    y = pl.pallas_call(
        _mlstm_kernel,
        grid=(batch // nb, seq // L),
        in_specs=[col_spec(0), col_spec(1), col_spec(2),
                  pl.BlockSpec((nb, L, LANES), lambda b, c: (b, c, 0)),
                  const_spec(conv_w.shape), const_spec(conv_b.shape), const_spec(out_gain.shape)],
        out_specs=pl.BlockSpec((nb, L, d), lambda b, c: (b, c, 0)),
        out_shape=jax.ShapeDtypeStruct((batch, seq, d), BF16),
        scratch_shapes=[pltpu.VMEM((slots, MLSTM_QK_DIM, MLSTM_V_DIM), F32),
                        pltpu.VMEM((slots, SUBLANES, MLSTM_QK_DIM), F32),
                        pltpu.VMEM((slots, SUBLANES, LANES), F32),
                        pltpu.VMEM((nb, SUBLANES, d), F32)],
        compiler_params=_params("arbitrary", "arbitrary"),
        name="mlstm_cell",
    )(proj3, proj3, proj3, gates.reshape(batch, seq, LANES), conv_w, conv_b, out_gain)
    return y.reshape(batch * seq, d)


def _mlp_kernel(x_ref, g_ref, w1_ref, w2_ref, *rest, with_branch):
    if with_branch:
        y_ref, wb_ref, out_ref, h_ref = rest
    else:
        out_ref, h_ref = rest
    j = pl.program_id(1)

    @pl.when(j == 0)
    def _():
        x = x_ref[...]
        if with_branch:
            x = x + jnp.dot(y_ref[...], wb_ref[...], preferred_element_type=F32)
        h_ref[...] = _rms_rows(x, g_ref[...]).astype(BF16)
        out_ref[...] = x

    hid = jnp.maximum(jnp.dot(h_ref[...], w1_ref[...].astype(BF16), preferred_element_type=F32), 0.0)
    out_ref[...] += jnp.dot((hid * hid).astype(BF16), w2_ref[...].astype(BF16), preferred_element_type=F32)


def _mlp(x, g, w1_all, w2_all, layer, branch=None):
    t, d = x.shape
    f = w1_all.shape[2]
    in_specs = [
        pl.BlockSpec((ROW_TILE, d), lambda i, j: (i, 0)),
        pl.BlockSpec((1, d), lambda i, j: (0, 0)),
        pl.BlockSpec((None, d, COL_TILE), lambda i, j: (layer, 0, j)),
        pl.BlockSpec((None, COL_TILE, d), lambda i, j: (layer, j, 0)),
    ]
    operands = [x, g, w1_all, w2_all]
    if branch is not None:
        in_specs += [pl.BlockSpec((ROW_TILE, d), lambda i, j: (i, 0)),
                     pl.BlockSpec((d, d), lambda i, j: (0, 0))]
        operands += list(branch)
    return pl.pallas_call(
        partial(_mlp_kernel, with_branch=branch is not None),
        grid=(t // ROW_TILE, f // COL_TILE),
        in_specs=in_specs,
        out_specs=pl.BlockSpec((ROW_TILE, d), lambda i, j: (i, 0)),
        out_shape=jax.ShapeDtypeStruct((t, d), F32),
        scratch_shapes=[pltpu.VMEM((ROW_TILE, d), BF16)],
        compiler_params=_params("parallel", "arbitrary"),
        name="mlp",
    )(*operands)


def _attention_layer(x, batch, seq, norm_g, w_in, q_gain, k_gain, w_out, rel_bias):
    gw = ATTN_HEADS * ATTN_HEAD_DIM
    gains = jnp.concatenate([
        jnp.tile(q_gain.astype(F32) * (ATTN_HEAD_DIM ** -0.5 * LOG2_E), (1, ATTN_HEADS)).reshape(-1),
        jnp.tile(k_gain.astype(F32), (1, ATTN_HEADS)).reshape(-1),
        jnp.ones((N_GROUPS * gw,), F32)])[None, :]
    head_of_lane = jnp.arange(MXU_DIM) // ATTN_HEAD_DIM
    seg = ((head_of_lane[:, None] == head_of_lane[None, :]).astype(F32) * (1.0 / ATTN_HEAD_DIM)).astype(BF16)
    qkv = _attn_proj(x, norm_g[None, :], w_in.astype(BF16), gains, seg)

    outs, lses, dens = [], [], []
    for g, (_, dilation) in enumerate(DILATION_PATTERNS):
        o, l, dn = _attention_group(qkv, _bias_table(rel_bias[:, g], dilation), g, batch, seq)
        outs.append(o)
        lses.append(l)
        dens.append(dn)

    head_rows = jnp.arange(LANES)[:, None]
    head_cols = (jnp.arange(gw) // ATTN_HEAD_DIM)[None, :]
    spread = (head_rows == head_cols).astype(BF16)
    expand = jnp.concatenate([spread, spread], axis=0)
    return _attn_out(x, outs, lses, dens, expand, w_out.astype(BF16))


def _mlstm_layer(x, batch, seq, norm_g, w_in_all, layer, gate_bias, conv_w, conv_b, out_gain, w_out):
    main = 2 * MLSTM_HEADS * MLSTM_QK_DIM + MLSTM_HEADS * MLSTM_V_DIM + D_MODEL
    n_gates = 2 * MLSTM_HEADS
    wg = jnp.pad(w_in_all[layer, :, main:], ((0, 0), (0, LANES - n_gates))).astype(BF16)
    gb = jnp.pad(gate_bias.astype(F32), (0, LANES - n_gates))[None, :]
    proj, gates = _mlstm_proj(x, norm_g[None, :], w_in_all, layer, main, wg, gb)
    y = _mlstm_cell(proj, gates, conv_w.astype(F32), conv_b.astype(F32)[None, :],
                    out_gain.astype(F32).reshape(1, -1), batch, seq)
    return y, w_out.astype(BF16)


def kernel(x, mixer_norm, mlp_norm, rel_bias, attn_w_in, attn_q_gain, attn_k_gain, attn_w_out,
           mlstm_w_in, mlstm_gate_bias, mlstm_conv_w, mlstm_conv_b, mlstm_out_gain, mlstm_w_out,
           mlp_w_in, mlp_w_out):
    batch, seq, d = x.shape
    depth = mixer_norm.shape[0]
    h = x.reshape(batch * seq, d)
    for layer in range(depth):
        j = layer // 2
        branch = None
        if layer % 2 == 0:
            h = _attention_layer(h, batch, seq, mixer_norm[layer], attn_w_in[j], attn_q_gain[j],
                                 attn_k_gain[j], attn_w_out[j], rel_bias)
        else:
            branch = _mlstm_layer(h, batch, seq, mixer_norm[layer], mlstm_w_in, j, mlstm_gate_bias[j],
                                  mlstm_conv_w[j], mlstm_conv_b[j], mlstm_out_gain[j], mlstm_w_out[j])
        h = _mlp(h, mlp_norm[layer][None, :], mlp_w_in, mlp_w_out, layer, branch)
    return h.reshape(batch, seq, d)
```

```python
import math
from functools import partial

import jax
import jax.numpy as jnp
from jax import lax
from jax.experimental import pallas as pl
from jax.experimental.pallas import tpu as pltpu

F32 = jnp.float32
BF16 = jnp.bfloat16

D_MODEL = 1024
EPS = 1e-6
LOG2_E = math.log2(math.e)
LN_2 = math.log(2.0)

ATTN_HEADS = 16
ATTN_HEAD_DIM = 64
DILATION_PATTERNS = ((128, 1), (512, 4), (2048, 16))
N_GROUPS = len(DILATION_PATTERNS)
BAND = 128
REL_BUCKETS = 32
REL_MAX_DISTANCE = 2048
MASKED = -1e30
PERM_TILE = 512
ATTN_BLOCKS_PER_STEP = 4

MLSTM_HEADS = 4
MLSTM_V_DIM = 256
MLSTM_QK_DIM = 128
CONV_WIDTH = 4
MLSTM_CHUNK = 256
MLSTM_BATCH_PER_STEP = 1

LANES = 128
SUBLANES = 8
MXU_DIM = 256
VMEM_LIMIT_BYTES = 56 * 1024 * 1024

ROW_TILE = 1024
COL_TILE = 1024
N_SLABS = D_MODEL // LANES


def _params(*semantics):
    return pltpu.CompilerParams(dimension_semantics=semantics, vmem_limit_bytes=VMEM_LIMIT_BYTES)


def _rms_rows(x, gain):
    ms = jnp.mean(x * x, axis=-1, keepdims=True)
    return x * lax.rsqrt(ms + EPS) * gain


def _sigmoid(x):
    return 0.5 * jnp.tanh(0.5 * x) + 0.5


def _log_sigmoid(x):
    return jnp.minimum(x, 0.0) - jnp.log1p(jnp.exp(-jnp.abs(x)))


def _attn_proj_kernel(x_ref, g_ref, w_ref, gain_ref, seg_ref, o_ref,
                      xn_ref, s1_ref, h_ref, acc_a_ref, acc_b_ref, *, n_col_tiles, n_norm_tiles, n_steps):
    t = pl.program_id(0)
    j = t % n_col_tiles
    d1 = DILATION_PATTERNS[1][1]
    d2 = DILATION_PATTERNS[2][1]
    assert DILATION_PATTERNS[0][1] == 1 and d2 == d1 * d1

    @pl.when(jnp.logical_and(j == 0, t < n_steps - 1))
    def _():
        xn = _rms_rows(x_ref[...], g_ref[...])
        h_ref[0] = xn.astype(BF16)
        for c in range(N_SLABS):
            xn_ref[c] = xn[:, c * LANES:(c + 1) * LANES]
        rows1 = PERM_TILE // d1
        rows2 = PERM_TILE // d2
        for base in range(0, ROW_TILE, PERM_TILE):
            for r in range(d1):
                dst = slice(base + r * rows1, base + (r + 1) * rows1)
                for c in range(N_SLABS):
                    blk = xn_ref[c, pl.ds(base + r, rows1, stride=d1), :]
                    s1_ref[c, dst, :] = blk
                    h_ref[1, dst, c * LANES:(c + 1) * LANES] = blk.astype(BF16)
            for r1 in range(d1):
                for r2 in range(d1):
                    r = r1 + d1 * r2
                    dst = slice(base + r * rows2, base + (r + 1) * rows2)
                    for c in range(N_SLABS):
                        blk = s1_ref[c, pl.ds(base + r1 * rows1 + r2, rows2, stride=d1), :]
                        h_ref[2, dst, c * LANES:(c + 1) * LANES] = blk.astype(BF16)

    @pl.when(t == 0)
    def _():
        acc_b_ref[...] = jnp.zeros_like(acc_b_ref)

    def step(acc_w_ref, acc_r_ref, normalise):
        acc_w_ref[...] = jnp.dot(h_ref[j % N_GROUPS], w_ref[...], preferred_element_type=F32)
        prev = acc_r_ref[...]
        if not normalise:
            o_ref[...] = prev.astype(BF16)
            return
        sq = (prev * prev).astype(BF16)
        ms = jnp.concatenate(
            [jnp.dot(sq[:, c:c + MXU_DIM], seg_ref[...], preferred_element_type=F32)
             for c in range(0, COL_TILE, MXU_DIM)], axis=1)
        o_ref[...] = (prev * lax.rsqrt(ms + EPS) * gain_ref[...]).astype(BF16)

    prev_is_qk = (jnp.maximum(t - 1, 0) % n_col_tiles) < n_norm_tiles
    for parity, (acc_w_ref, acc_r_ref) in enumerate(((acc_a_ref, acc_b_ref), (acc_b_ref, acc_a_ref))):
        for normalise in (True, False):
            @pl.when(jnp.logical_and(t % 2 == parity, prev_is_qk == normalise))
            def _(acc_w_ref=acc_w_ref, acc_r_ref=acc_r_ref, normalise=normalise):
                step(acc_w_ref, acc_r_ref, normalise)


def _attn_proj(x, g, w, gains, seg):
    t, d = x.shape
    n = w.shape[1]
    n_col_tiles = n // COL_TILE
    n_row_tiles = t // ROW_TILE
    n_steps = n_row_tiles * n_col_tiles + 1
    n_norm_tiles = 2 * N_GROUPS * ATTN_HEADS * ATTN_HEAD_DIM // COL_TILE

    prev = lambda s: jnp.maximum(s - 1, 0)
    prev_col = lambda s: prev(s) % n_col_tiles
    return pl.pallas_call(
        partial(_attn_proj_kernel, n_col_tiles=n_col_tiles, n_norm_tiles=n_norm_tiles, n_steps=n_steps),
        grid=(n_steps,),
        in_specs=[
            pl.BlockSpec((ROW_TILE, d), lambda s: (jnp.minimum(s // n_col_tiles, n_row_tiles - 1), 0)),
            pl.BlockSpec((1, d), lambda s: (0, 0)),
            pl.BlockSpec((d, COL_TILE), lambda s: (0, s % n_col_tiles)),
            pl.BlockSpec((1, COL_TILE), lambda s: (0, prev_col(s))),
            pl.BlockSpec((MXU_DIM, MXU_DIM), lambda s: (0, 0)),
        ],
        out_specs=pl.BlockSpec((ROW_TILE, COL_TILE), lambda s: (prev(s) // n_col_tiles, prev_col(s))),
        out_shape=jax.ShapeDtypeStruct((t, n), BF16),
        scratch_shapes=[pltpu.VMEM((N_SLABS, ROW_TILE, LANES), F32),
                        pltpu.VMEM((N_SLABS, ROW_TILE, LANES), F32),
                        pltpu.VMEM((N_GROUPS, ROW_TILE, d), BF16),
                        pltpu.VMEM((ROW_TILE, COL_TILE), F32),
                        pltpu.VMEM((ROW_TILE, COL_TILE), F32)],
        compiler_params=_params("arbitrary"),
        name="attn_proj",
    )(x, g, w, gains, seg)


def _attn_kernel(q_ref, k_ref, v_ref, bias_ref, o_ref, lse_ref, den_ref, kprev_ref, vprev_ref):
    n = pl.program_id(2)
    gw = ATTN_HEADS * ATTN_HEAD_DIM

    @pl.when(n == 0)
    def _():
        kprev_ref[...] = jnp.zeros_like(kprev_ref)
        vprev_ref[...] = jnp.zeros_like(vprev_ref)

    first = (n == 0).astype(jnp.int32)
    heads_per_pass = MXU_DIM // ATTN_HEAD_DIM
    lane_head = lax.shift_right_logical(
        lax.broadcasted_iota(jnp.int32, (BAND, MXU_DIM), 1), int(math.log2(ATTN_HEAD_DIM)))
    low_half = lax.broadcasted_iota(jnp.int32, (BAND, LANES), 1) < ATTN_HEAD_DIM
    lse_lane = lax.broadcasted_iota(jnp.int32, (BAND, LANES), 1)
    n_blocks = q_ref.shape[0] * q_ref.shape[1] // BAND
    step_rows = n_blocks * BAND
    max_tiles = [jnp.zeros((BAND, LANES), F32) for _ in range(n_blocks)]
    den_tiles = [jnp.ones((BAND, LANES), F32) for _ in range(n_blocks)]

    for c in range(ATTN_HEADS // heads_per_pass):
        cols = slice(c * MXU_DIM, (c + 1) * MXU_DIM)
        q_all = q_ref[:, :, cols].reshape(step_rows, MXU_DIM)
        k_all = k_ref[:, :, cols].reshape(step_rows, MXU_DIM)
        v_all = v_ref[:, :, cols].reshape(step_rows, MXU_DIM)
        k_prev = kprev_ref[:, cols]
        v_prev = vprev_ref[:, cols]
        kprev_ref[:, cols] = k_all[step_rows - BAND:]
        vprev_ref[:, cols] = v_all[step_rows - BAND:]
        outs = []
        for blk in range(n_blocks):
            rows = slice(blk * BAND, (blk + 1) * BAND)
            qq = q_all[rows]
            kk = jnp.concatenate([k_prev, k_all[rows]], axis=0)
            vv = jnp.concatenate([v_prev, v_all[rows]], axis=0)
            k_prev, v_prev = k_all[rows], v_all[rows]
            table = first if blk == 0 else 0
            lhs = jnp.concatenate(
                [jnp.where(lane_head == hh, qq, jnp.zeros_like(qq)) for hh in range(heads_per_pass)], axis=0)
            s_all = lax.dot_general(lhs, kk, (((1,), (1,)), ((), ())), preferred_element_type=F32)
            probs = []
            for hh in range(heads_per_pass):
                h = c * heads_per_pass + hh
                s = s_all[hh * BAND:(hh + 1) * BAND] + bias_ref[table, h]
                m = jnp.max(s, axis=-1, keepdims=True)
                p = jnp.exp2(s - m)
                den = jnp.sum(p, axis=-1, keepdims=True)
                probs.append(p.astype(BF16))
                max_tiles[blk] = jnp.where(lse_lane == h, m, max_tiles[blk])
                den_tiles[blk] = jnp.where(lse_lane == h, den, den_tiles[blk])
            o_all = jnp.dot(jnp.concatenate(probs, axis=0), vv, preferred_element_type=F32)
            halves = []
            for half in range(MXU_DIM // LANES):
                lanes = slice(half * LANES, (half + 1) * LANES)
                h0 = 2 * half
                halves.append(jnp.where(low_half, o_all[h0 * BAND:(h0 + 1) * BAND, lanes],
                                        o_all[(h0 + 1) * BAND:(h0 + 2) * BAND, lanes]))
            outs.append(jnp.concatenate(halves, axis=1).astype(BF16))
        o_ref[:, :, cols] = jnp.concatenate(outs, axis=0).reshape(o_ref.shape[0], o_ref.shape[1], MXU_DIM)

    den = jnp.concatenate(den_tiles, axis=0)
    lse = (jnp.concatenate(max_tiles, axis=0) + jnp.log2(den)) * LN_2
    lse_ref[...] = lse.reshape(lse_ref.shape)
    den_ref[...] = den.reshape(den_ref.shape)


def _attention_group(qkv, bias, group, batch, seq):
    dilation = DILATION_PATTERNS[group][1]
    gw = ATTN_HEADS * ATTN_HEAD_DIM
    tiles_per_seq = seq // PERM_TILE
    rows_per_residue = PERM_TILE // dilation
    n_tiles = batch * tiles_per_seq

    step_rows = ATTN_BLOCKS_PER_STEP * BAND
    if rows_per_residue >= step_rows:
        per = rows_per_residue // step_rows
        lead, rows = 1, step_rows
        row_idx = lambda b, r, n: (b * tiles_per_seq + n // per, r * per + n % per)
    else:
        lead, rows = step_rows // rows_per_residue, rows_per_residue
        row_idx = lambda b, r, n: (b * (tiles_per_seq // lead) + n, r)

    def spec(width, col):
        return pl.BlockSpec((lead, rows, width), lambda b, r, n: (*row_idx(b, r, n), col))

    view = qkv.reshape(n_tiles, PERM_TILE, qkv.shape[1])
    out, lse, den = pl.pallas_call(
        _attn_kernel,
        grid=(batch, dilation, seq // dilation // step_rows),
        in_specs=[spec(gw, group), spec(gw, N_GROUPS + group), spec(gw, 2 * N_GROUPS + group),
                  pl.BlockSpec(bias.shape, lambda b, r, n: (0, 0, 0, 0))],
        out_specs=[spec(gw, 0), spec(LANES, 0), spec(LANES, 0)],
        out_shape=[jax.ShapeDtypeStruct((n_tiles, PERM_TILE, gw), BF16),
                   jax.ShapeDtypeStruct((n_tiles, PERM_TILE, LANES), F32),
                   jax.ShapeDtypeStruct((n_tiles, PERM_TILE, LANES), F32)],
        scratch_shapes=[pltpu.VMEM((BAND, gw), BF16), pltpu.VMEM((BAND, gw), BF16)],
        compiler_params=_params("arbitrary", "arbitrary", "arbitrary"),
        name=f"attn_group{group}",
    )(view, view, view, bias)
    return (out.reshape(batch * seq, gw), lse.reshape(batch * seq, LANES),
            den.reshape(batch * seq, LANES))


def _t5_bucket(dist):
    max_exact = REL_BUCKETS // 2
    d = jnp.maximum(dist.astype(F32), 1.0)
    large = max_exact + (jnp.log(d / max_exact) / math.log(REL_MAX_DISTANCE / max_exact)
                         * (REL_BUCKETS - max_exact)).astype(jnp.int32)
    large = jnp.minimum(large, REL_BUCKETS - 1)
    return jnp.where(dist < max_exact, dist, large)


def _bias_table(rel_bias_g, dilation):
    qi = jnp.arange(BAND)[:, None]
    ki = jnp.arange(2 * BAND)[None, :]
    steps = qi - ki + BAND
    valid = (steps >= 0) & (steps <= BAND)
    bucket = _t5_bucket(jnp.clip(steps, 0, BAND) * dilation)
    onehot = bucket[None] == jnp.arange(REL_BUCKETS)[:, None, None]
    bias = jnp.sum(jnp.where(onehot[:, None], rel_bias_g.astype(F32)[:, :, None, None], 0.0), axis=0)
    bias = bias * LOG2_E
    regular = jnp.where(valid[None], bias, MASKED)
    first = jnp.where((valid & (ki >= BAND))[None], bias, MASKED)
    return jnp.stack([regular, first])


def _attn_out_kernel(o0_ref, o1_ref, o2_ref, l0_ref, l1_ref, l2_ref, d0_ref, d1_ref, d2_ref,
                     expand_ref, out_ref, o_scr, l_scr, d_scr):
    lses, dens, outs = [], [], []
    group_refs = ((o0_ref, l0_ref, d0_ref), (o1_ref, l1_ref, d1_ref), (o2_ref, l2_ref, d2_ref))
    for g, (o_ref, l_ref, d_ref) in enumerate(group_refs):
        dil = DILATION_PATTERNS[g][1]
        if dil == 1:
            outs.append(o_ref[...].astype(F32))
            lses.append(l_ref[...])
            dens.append(d_ref[...])
            continue
        rows = PERM_TILE // dil
        for r in range(dil):
            src = slice(r * rows, (r + 1) * rows)
            o_res = o_ref[src, :].astype(F32)
            for c in range(N_SLABS):
                o_scr[g - 1, c, pl.ds(r, rows, stride=dil), :] = o_res[:, c * LANES:(c + 1) * LANES]
            l_scr[g - 1, pl.ds(r, rows, stride=dil), :] = l_ref[src, :]
            d_scr[g - 1, pl.ds(r, rows, stride=dil), :] = d_ref[src, :]
        outs.append(jnp.concatenate([o_scr[g - 1, c] for c in range(N_SLABS)], axis=1))
        lses.append(l_scr[g - 1])
        dens.append(d_scr[g - 1])
    m = jnp.maximum(jnp.maximum(lses[0], lses[1]), lses[2])
    es = [jnp.exp(l - m) for l in lses]
    inv = 1.0 / (es[0] + es[1] + es[2])
    y = None
    for g, e in enumerate(es):
        wt = e * inv / dens[g]
        hi = wt.astype(BF16)
        lo = (wt - hi.astype(F32)).astype(BF16)
        wide = jnp.dot(jnp.concatenate([hi, lo], axis=1), expand_ref[...], preferred_element_type=F32)
        term = wide * outs[g]
        y = term if y is None else y + term
    out_ref[...] = y.astype(BF16)


def _attn_out(outs, lses, dens, expand):
    t, d = outs[0].shape
    rows = PERM_TILE
    row_spec = lambda width: pl.BlockSpec((rows, width), lambda i: (i, 0))
    const_spec = lambda shape: pl.BlockSpec(shape, lambda i: (0, 0))
    return pl.pallas_call(
        _attn_out_kernel,
        grid=(t // rows,),
        in_specs=[row_spec(d)] * N_GROUPS + [row_spec(LANES)] * (2 * N_GROUPS)
                 + [const_spec(expand.shape)],
        out_specs=row_spec(d),
        out_shape=jax.ShapeDtypeStruct((t, d), BF16),
        scratch_shapes=[pltpu.VMEM((N_GROUPS - 1, N_SLABS, rows, LANES), F32),
                        pltpu.VMEM((N_GROUPS - 1, rows, LANES), F32),
                        pltpu.VMEM((N_GROUPS - 1, rows, LANES), F32)],
        compiler_params=_params("parallel"),
        name="attn_out",
    )(*outs, *lses, *dens, expand)


def _mlstm_proj_kernel(x_ref, g_ref, w_ref, wg_ref, gb_ref, o_ref, gates_ref, h_ref):
    j = pl.program_id(1)

    @pl.when(j == 0)
    def _():
        h = _rms_rows(x_ref[...], g_ref[...]).astype(BF16)
        h_ref[...] = h
        gates_ref[...] = jnp.dot(h, wg_ref[...], preferred_element_type=F32) + gb_ref[...]

    o_ref[...] = jnp.dot(h_ref[...], w_ref[...].astype(BF16), preferred_element_type=F32).astype(o_ref.dtype)


def _mlstm_proj(x, g, w_all, layer, n, wg, gb):
    t, d = x.shape
    return pl.pallas_call(
        _mlstm_proj_kernel,
        grid=(t // ROW_TILE, n // COL_TILE),
        in_specs=[
            pl.BlockSpec((ROW_TILE, d), lambda i, j: (i, 0)),
            pl.BlockSpec((1, d), lambda i, j: (0, 0)),
            pl.BlockSpec((None, d, COL_TILE), lambda i, j: (layer, 0, j)),
            pl.BlockSpec((d, LANES), lambda i, j: (0, 0)),
            pl.BlockSpec((1, LANES), lambda i, j: (0, 0)),
        ],
        out_specs=[pl.BlockSpec((ROW_TILE, COL_TILE), lambda i, j: (i, j)),
                   pl.BlockSpec((ROW_TILE, LANES), lambda i, j: (i, 0))],
        out_shape=[jax.ShapeDtypeStruct((t, n), BF16), jax.ShapeDtypeStruct((t, LANES), F32)],
        scratch_shapes=[pltpu.VMEM((ROW_TILE, d), BF16)],
        compiler_params=_params("parallel", "arbitrary"),
        name="mlstm_proj",
    )(x, g, w_all, wg, gb)


def _mlstm_kernel(qk_ref, v_ref, op_ref, gates_ref, cw_ref, cb_ref, og_ref, y_ref,
                  c_scr, n_scr, m_scr, tail_scr):
    chunk = pl.program_id(1)
    L = MLSTM_CHUNK

    @pl.when(chunk == 0)
    def _():
        c_scr[...] = jnp.zeros_like(c_scr)
        n_scr[...] = jnp.zeros_like(n_scr)
        m_scr[...] = jnp.zeros_like(m_scr)
        tail_scr[...] = jnp.zeros_like(tail_scr)

    cw = cw_ref[...]
    lane_id = lax.broadcasted_iota(jnp.int32, (L, LANES), 1)
    row_id = lax.broadcasted_iota(jnp.int32, (LANES, L), 0)
    t_id = lax.broadcasted_iota(jnp.int32, (L, L), 0)
    s_id = lax.broadcasted_iota(jnp.int32, (L, L), 1)
    causal = s_id <= t_id
    qk_width = MLSTM_HEADS * MLSTM_QK_DIM

    for bi in range(qk_ref.shape[0]):
        xqk = qk_ref[bi].astype(F32)
        ext = jnp.concatenate([tail_scr[bi], xqk], axis=0)
        conv = xqk * cw[CONV_WIDTH - 1:CONV_WIDTH] + cb_ref[...]
        for back in range(1, CONV_WIDTH):
            shifted = pltpu.roll(ext, back, axis=0)[SUBLANES:]
            conv = conv + shifted * cw[CONV_WIDTH - 1 - back:CONV_WIDTH - back]
        tail_scr[bi] = xqk[L - SUBLANES:]
        act = conv * _sigmoid(conv)

        gates = gates_ref[bi]
        log_f = _log_sigmoid(gates)
        gates_t = gates.T
        log_f_t = log_f.T

        for h in range(MLSTM_HEADS):
            slot = bi * MLSTM_HEADS + h
            q = act[:, h * MLSTM_QK_DIM:(h + 1) * MLSTM_QK_DIM] * (MLSTM_QK_DIM ** -0.5)
            k = act[:, qk_width + h * MLSTM_QK_DIM:qk_width + (h + 1) * MLSTM_QK_DIM]
            vcols = slice(h * MLSTM_V_DIM, (h + 1) * MLSTM_V_DIM)
            v = v_ref[bi, :, vcols]
            i_col = jnp.sum(jnp.where(lane_id == h, gates, 0.0), axis=1, keepdims=True)
            f_col = jnp.sum(jnp.where(lane_id == MLSTM_HEADS + h, log_f, 0.0), axis=1, keepdims=True)
            i_row = jnp.sum(jnp.where(row_id == h, gates_t, 0.0), axis=0, keepdims=True)
            f_row = jnp.sum(jnp.where(row_id == MLSTM_HEADS + h, log_f_t, 0.0), axis=0, keepdims=True)
            bcum_col = jnp.sum(jnp.where(causal, f_row, 0.0), axis=1, keepdims=True)
            bcum_row = jnp.sum(jnp.where(t_id <= s_id, f_col, 0.0), axis=0, keepdims=True)
            b_last = jnp.sum(f_col, axis=0, keepdims=True)
            m_prev = m_scr[slot][0:1, 0:1]
            n_prev = n_scr[slot][0:1, :]
            c_prev = c_scr[slot]

            dmat = jnp.where(causal, bcum_col - bcum_row + i_row, -jnp.inf)
            m_t = jnp.maximum(bcum_col + m_prev, jnp.max(dmat, axis=1, keepdims=True))
            inter = jnp.exp(bcum_col + m_prev - m_t)
            qb = q.astype(BF16)
            kb = k.astype(BF16)
            scores = lax.dot_general(qb, kb, (((1,), (1,)), ((), ())), preferred_element_type=F32)
            wts = scores * jnp.exp(dmat - m_t)
            num = (inter * jnp.dot(qb, c_prev.astype(BF16), preferred_element_type=F32)
                   + jnp.dot(wts.astype(BF16), v, preferred_element_type=F32))
            den = (inter * jnp.sum(q * n_prev, axis=1, keepdims=True)
                   + jnp.sum(wts, axis=1, keepdims=True))
            inv_den = 1.0 / jnp.maximum(jnp.abs(den), jnp.exp(-m_t))
            ms = (inv_den * inv_den) * jnp.mean(num * num, axis=1, keepdims=True)
            row_scale = inv_den * lax.rsqrt(ms + EPS)

            a_col = b_last - bcum_col + i_col
            m_new = jnp.maximum(b_last + m_prev, jnp.max(a_col, axis=0, keepdims=True))
            decay = jnp.exp(b_last + m_prev - m_new)
            kw = jnp.exp(a_col - m_new) * k
            c_scr[slot] = decay * c_prev + jnp.dot(kw.T.astype(BF16), v, preferred_element_type=F32)
            n_new = decay * n_prev + jnp.sum(kw, axis=0, keepdims=True)
            n_scr[slot] = jnp.broadcast_to(n_new, (SUBLANES, MLSTM_QK_DIM))
            m_scr[slot] = jnp.broadcast_to(m_new, (SUBLANES, LANES))

            gated = num * row_scale * og_ref[:, vcols] * _sigmoid(op_ref[bi, :, vcols].astype(F32))
            y_ref[bi, :, vcols] = gated.astype(y_ref.dtype)


def _mlstm_cell(proj, gates, conv_w, conv_b, out_gain, batch, seq):
    L = MLSTM_CHUNK
    nb = MLSTM_BATCH_PER_STEP
    d = D_MODEL
    slots = nb * MLSTM_HEADS
    proj3 = proj.reshape(batch, seq, proj.shape[1])
    col_spec = lambda which: pl.BlockSpec((nb, L, d), lambda b, c: (b, c, which))
    const_spec = lambda shape: pl.BlockSpec(shape, lambda b, c: (0, 0))
    y = pl.pallas_call(
        _mlstm_kernel,
        grid=(batch // nb, seq // L),
        in_specs=[col_spec(0), col_spec(1), col_spec(2),
                  pl.BlockSpec((nb, L, LANES), lambda b, c: (b, c, 0)),
                  const_spec(conv_w.shape), const_spec(conv_b.shape), const_spec(out_gain.shape)],
        out_specs=pl.BlockSpec((nb, L, d), lambda b, c: (b, c, 0)),
        out_shape=jax.ShapeDtypeStruct((batch, seq, d), BF16),
        scratch_shapes=[pltpu.VMEM((slots, MLSTM_QK_DIM, MLSTM_V_DIM), F32),
                        pltpu.VMEM((slots, SUBLANES, MLSTM_QK_DIM), F32),
                        pltpu.VMEM((slots, SUBLANES, LANES), F32),
                        pltpu.VMEM((nb, SUBLANES, d), F32)],
        compiler_params=_params("arbitrary", "arbitrary"),
        name="mlstm_cell",
    )(proj3, proj3, proj3, gates.reshape(batch, seq, LANES), conv_w, conv_b, out_gain)
    return y.reshape(batch * seq, d)


def _mlp_kernel(x_ref, g_ref, w1_ref, w2_ref, *rest, with_branch):
    if with_branch:
        y_ref, wb_ref, out_ref, h_ref = rest
    else:
        out_ref, h_ref = rest
    j = pl.program_id(1)

    @pl.when(j == 0)
    def _():
        x = x_ref[...]
        if with_branch:
            x = x + jnp.dot(y_ref[...], wb_ref[...], preferred_element_type=F32)
        h_ref[...] = _rms_rows(x, g_ref[...]).astype(BF16)
        out_ref[...] = x

    hid = jnp.maximum(jnp.dot(h_ref[...], w1_ref[...].astype(BF16), preferred_element_type=F32), 0.0)
    out_ref[...] += jnp.dot((hid * hid).astype(BF16), w2_ref[...].astype(BF16), preferred_element_type=F32)


def _mlp(x, g, w1_all, w2_all, layer, branch=None):
    t, d = x.shape
    f = w1_all.shape[2]
    in_specs = [
        pl.BlockSpec((ROW_TILE, d), lambda i, j: (i, 0)),
        pl.BlockSpec((1, d), lambda i, j: (0, 0)),
        pl.BlockSpec((None, d, COL_TILE), lambda i, j: (layer, 0, j)),
        pl.BlockSpec((None, COL_TILE, d), lambda i, j: (layer, j, 0)),
    ]
    operands = [x, g, w1_all, w2_all]
    if branch is not None:
        in_specs += [pl.BlockSpec((ROW_TILE, d), lambda i, j: (i, 0)),
                     pl.BlockSpec((d, d), lambda i, j: (0, 0))]
        operands += list(branch)
    return pl.pallas_call(
        partial(_mlp_kernel, with_branch=branch is not None),
        grid=(t // ROW_TILE, f // COL_TILE),
        in_specs=in_specs,
        out_specs=pl.BlockSpec((ROW_TILE, d), lambda i, j: (i, 0)),
        out_shape=jax.ShapeDtypeStruct((t, d), F32),
        scratch_shapes=[pltpu.VMEM((ROW_TILE, d), BF16)],
        compiler_params=_params("parallel", "arbitrary"),
        name="mlp",
    )(*operands)


def _attention_layer(x, batch, seq, norm_g, w_in, q_gain, k_gain, w_out, rel_bias):
    gw = ATTN_HEADS * ATTN_HEAD_DIM
    gains = jnp.concatenate([
        jnp.tile(q_gain.astype(F32) * (ATTN_HEAD_DIM ** -0.5 * LOG2_E), (1, ATTN_HEADS)).reshape(-1),
        jnp.tile(k_gain.astype(F32), (1, ATTN_HEADS)).reshape(-1),
        jnp.ones((N_GROUPS * gw,), F32)])[None, :]
    head_of_lane = jnp.arange(MXU_DIM) // ATTN_HEAD_DIM
    seg = ((head_of_lane[:, None] == head_of_lane[None, :]).astype(F32) * (1.0 / ATTN_HEAD_DIM)).astype(BF16)
    qkv = _attn_proj(x, norm_g[None, :], w_in.astype(BF16), gains, seg)

    outs, lses, dens = [], [], []
    for g, (_, dilation) in enumerate(DILATION_PATTERNS):
        o, l, dn = _attention_group(qkv, _bias_table(rel_bias[:, g], dilation), g, batch, seq)
        outs.append(o)
        lses.append(l)
        dens.append(dn)

    head_rows = jnp.arange(LANES)[:, None]
    head_cols = (jnp.arange(gw) // ATTN_HEAD_DIM)[None, :]
    spread = (head_rows == head_cols).astype(BF16)
    expand = jnp.concatenate([spread, spread], axis=0)
    return _attn_out(outs, lses, dens, expand), w_out.astype(BF16)


def _mlstm_layer(x, batch, seq, norm_g, w_in_all, layer, gate_bias, conv_w, conv_b, out_gain, w_out):
    main = 2 * MLSTM_HEADS * MLSTM_QK_DIM + MLSTM_HEADS * MLSTM_V_DIM + D_MODEL
    n_gates = 2 * MLSTM_HEADS
    wg = jnp.pad(w_in_all[layer, :, main:], ((0, 0), (0, LANES - n_gates))).astype(BF16)
    gb = jnp.pad(gate_bias.astype(F32), (0, LANES - n_gates))[None, :]
    proj, gates = _mlstm_proj(x, norm_g[None, :], w_in_all, layer, main, wg, gb)
    y = _mlstm_cell(proj, gates, conv_w.astype(F32), conv_b.astype(F32)[None, :],
                    out_gain.astype(F32).reshape(1, -1), batch, seq)
    return y, w_out.astype(BF16)


def kernel(x, mixer_norm, mlp_norm, rel_bias, attn_w_in, attn_q_gain, attn_k_gain, attn_w_out,
           mlstm_w_in, mlstm_gate_bias, mlstm_conv_w, mlstm_conv_b, mlstm_out_gain, mlstm_w_out,
           mlp_w_in, mlp_w_out):
    batch, seq, d = x.shape
    depth = mixer_norm.shape[0]
    h = x.reshape(batch * seq, d)
    for layer in range(depth):
        j = layer // 2
        if layer % 2 == 0:
            branch = _attention_layer(h, batch, seq, mixer_norm[layer], attn_w_in[j], attn_q_gain[j],
                                      attn_k_gain[j], attn_w_out[j], rel_bias)
        else:
            branch = _mlstm_layer(h, batch, seq, mixer_norm[layer], mlstm_w_in, j, mlstm_gate_bias[j],
                                  mlstm_conv_w[j], mlstm_conv_b[j], mlstm_out_gain[j], mlstm_w_out[j])
        h = _mlp(h, mlp_norm[layer][None, :], mlp_w_in, mlp_w_out, layer, branch)
    return h.reshape(batch, seq, d)
```
